```python
import jax, jax.numpy as jnp
from jax import lax
import numpy as np

D_MODEL = 1024
BATCH = 4
SEQ = 8192
DEPTH = 2

D_MIX = D_MODEL
NORM_EPS = 1e-6
ATT_HEADS = 4
QK_NOPE_DIM = 128
QK_ROPE_DIM = 64
V_HEAD_DIM = 128
ATT_WIDTH = ATT_HEADS * V_HEAD_DIM
Q_LORA_RANK = D_MODEL // 4
KV_LORA_RANK = D_MODEL // 8
ROPE_THETA = 10000.0
Q_BLOCK = 128
CONV_WIDTH = D_MIX // 4
CONV_KERNEL = 31
SGU_WIDTH = D_MIX // 4
SGU_GROUPS = 4
SGU_GROUP_DIM = SGU_WIDTH // SGU_GROUPS
SGU_CHUNK = 128
IN_SPLITS = (Q_LORA_RANK, KV_LORA_RANK, QK_ROPE_DIM, ATT_WIDTH,
             CONV_WIDTH, CONV_WIDTH, CONV_WIDTH,
             SGU_WIDTH, SGU_WIDTH, SGU_WIDTH)
D_IN = Q_LORA_RANK + KV_LORA_RANK + QK_ROPE_DIM + ATT_WIDTH + 3 * CONV_WIDTH + 3 * SGU_WIDTH

kernel_name = "hymba_style_mla_conformer_sgu_hybrid"


def rms_norm(x, g):
    xf = x.astype(jnp.float32)
    y = xf * lax.rsqrt(jnp.mean(xf * xf, axis=-1, keepdims=True) + NORM_EPS)
    return (y * g.astype(jnp.float32)).astype(x.dtype)


def layer_norm(x, g, b):
    xf = x.astype(jnp.float32)
    mu = jnp.mean(xf, axis=-1, keepdims=True)
    var = jnp.mean(jnp.square(xf - mu), axis=-1, keepdims=True)
    y = (xf - mu) * lax.rsqrt(var + NORM_EPS)
    return (y * g.astype(jnp.float32) + b.astype(jnp.float32)).astype(x.dtype)


def apply_rope(x, cos, sin):
    half = x.shape[-1] // 2
    x1, x2 = x[..., :half], x[..., half:]
    return jnp.concatenate([x1 * cos - x2 * sin, x2 * cos + x1 * sin], axis=-1)


def mla_mixer(q_lat, kv_lat, k_rope, positions, q_norm_g, w_uq, kv_norm_g, w_ukv):
    B, S, _ = q_lat.shape
    q = (rms_norm(q_lat, q_norm_g) @ w_uq).reshape(B, S, ATT_HEADS, QK_NOPE_DIM + QK_ROPE_DIM)
    q_nope, q_rope = q[..., :QK_NOPE_DIM], q[..., QK_NOPE_DIM:]
    kv = (rms_norm(kv_lat, kv_norm_g) @ w_ukv).reshape(B, S, ATT_HEADS, QK_NOPE_DIM + V_HEAD_DIM)
    k_nope, v = kv[..., :QK_NOPE_DIM], kv[..., QK_NOPE_DIM:]
    inv_freq = ROPE_THETA ** (-jnp.arange(0, QK_ROPE_DIM, 2, dtype=jnp.float32) / QK_ROPE_DIM)
    ang = positions.astype(jnp.float32)[..., None] * inv_freq
    cos, sin = jnp.cos(ang).astype(q.dtype), jnp.sin(ang).astype(q.dtype)
    q_rope = apply_rope(q_rope, cos[:, :, None, :], sin[:, :, None, :])
    k_rope = apply_rope(k_rope, cos, sin)
    scale = (QK_NOPE_DIM + QK_ROPE_DIM) ** -0.5
    nb = S // Q_BLOCK
    qn_blocks = q_nope.reshape(B, nb, Q_BLOCK, ATT_HEADS, QK_NOPE_DIM).transpose(1, 0, 2, 3, 4)
    qr_blocks = q_rope.reshape(B, nb, Q_BLOCK, ATT_HEADS, QK_ROPE_DIM).transpose(1, 0, 2, 3, 4)
    k_pos = jnp.arange(S)

    def one_block(args):
        qn, qr, blk = args
        s = (jnp.einsum('bqhd,bkhd->bhqk', qn, k_nope)
             + jnp.einsum('bqhr,bkr->bhqk', qr, k_rope)).astype(jnp.float32) * scale
        q_pos = blk * Q_BLOCK + jnp.arange(Q_BLOCK)
        s = jnp.where(k_pos[None, :] <= q_pos[:, None], s, -jnp.inf)
        p = jax.nn.softmax(s, axis=-1).astype(v.dtype)
        return jnp.einsum('bhqk,bkhd->bqhd', p, v)

    out = lax.map(one_block, (qn_blocks, qr_blocks, jnp.arange(nb)))
    return out.transpose(1, 0, 2, 3, 4).reshape(B, S, ATT_WIDTH)


def conv_mixer(a, b, conv_w, conv_b, ln_g, ln_b, w_pw2):
    h = a * jax.nn.sigmoid(b)
    h = lax.conv_general_dilated(
        h, conv_w[:, None, :].astype(h.dtype), window_strides=(1,),
        padding=[(CONV_KERNEL - 1, 0)],
        dimension_numbers=('NWC', 'WIO', 'NWC'),
        feature_group_count=CONV_WIDTH) + conv_b
    h = jax.nn.silu(layer_norm(h, ln_g, ln_b))
    return h @ w_pw2


def sgu_mixer(u, v, ln_g, ln_b, w_s, b_s):
    u = jax.nn.gelu(u, approximate=False)
    v = layer_norm(jax.nn.gelu(v, approximate=False), ln_g, ln_b)
    B, S, _ = v.shape
    nc = S // SGU_CHUNK
    v = v.reshape(B, nc, SGU_CHUNK, SGU_GROUPS, SGU_GROUP_DIM)
    causal = jnp.tril(jnp.ones((SGU_CHUNK, SGU_CHUNK), dtype=bool))
    w = jnp.where(causal[None], w_s, jnp.zeros_like(w_s))
    sv = jnp.einsum('gts,bnsgc->bntgc', w, v) + b_s.T[None, None, :, :, None]
    return u * sv.reshape(B, S, SGU_WIDTH)


def hybrid_layer(x, c, positions, w_ada, b_ada, g_pre, g_post, w_in, q_norm_g, w_uq,
                 kv_norm_g, w_ukv, conv_w, conv_b, conv_ln_g, conv_ln_b, w_pw2,
                 sgu_ln_g, sgu_ln_b, w_s, b_s, w_out):
    mod = jax.nn.silu(c) @ w_ada + b_ada
    shift, scale, gate = jnp.split(mod, 3, axis=-1)
    h = rms_norm(x, g_pre) * (1 + scale[:, None, :]) + shift[:, None, :]
    z = h @ w_in
    (q_lat, kv_lat, k_rope, g_att, conv_a, conv_b_in, g_conv,
     sgu_u, sgu_v, g_sgu) = jnp.split(z, np.cumsum(IN_SPLITS)[:-1].tolist(), axis=-1)
    y_att = mla_mixer(q_lat, kv_lat, k_rope, positions, q_norm_g, w_uq, kv_norm_g, w_ukv)
    y_conv = conv_mixer(conv_a, conv_b_in, conv_w, conv_b, conv_ln_g, conv_ln_b, w_pw2)
    y_sgu = sgu_mixer(sgu_u, sgu_v, sgu_ln_g, sgu_ln_b, w_s, b_s)
    y = jnp.concatenate([y_att * jax.nn.silu(g_att),
                         y_conv * jax.nn.silu(g_conv),
                         y_sgu * jax.nn.silu(g_sgu)], axis=-1) @ w_out
    return x + gate[:, None, :] * rms_norm(y, g_post)


def setup_inputs(seed: int = 0) -> dict:
    key = jax.random.key(seed)
    ks = jax.random.split(key, 32)
    f32 = jnp.float32
    L = DEPTH

    def nrm(k, shape, s):
        return jax.random.normal(k, shape, f32) * s

    def gain(k, shape):
        return 1.0 + 0.02 * jax.random.normal(k, shape, f32)

    b_ada = jnp.concatenate([nrm(ks[3], (L, D_MODEL), 0.02),
                             nrm(ks[4], (L, D_MODEL), 0.02),
                             gain(ks[5], (L, D_MODEL))], axis=-1)
    return {
        "x": jax.random.normal(ks[0], (BATCH, SEQ, D_MODEL), f32),
        "c": jax.random.normal(ks[1], (BATCH, D_MODEL), f32),
        "positions": jnp.tile(jnp.arange(SEQ, dtype=jnp.int32)[None, :], (BATCH, 1)),
        "w_ada": nrm(ks[2], (L, D_MODEL, 3 * D_MODEL), 0.1 * D_MODEL ** -0.5),
        "b_ada": b_ada,
        "g_pre": gain(ks[6], (L, D_MODEL)),
        "g_post": gain(ks[7], (L, D_MODEL)),
        "w_in": nrm(ks[8], (L, D_MODEL, D_IN), D_MODEL ** -0.5),
        "q_norm_g": gain(ks[9], (L, Q_LORA_RANK)),
        "w_uq": nrm(ks[10], (L, Q_LORA_RANK, ATT_HEADS * (QK_NOPE_DIM + QK_ROPE_DIM)), Q_LORA_RANK ** -0.5),
        "kv_norm_g": gain(ks[11], (L, KV_LORA_RANK)),
        "w_ukv": nrm(ks[12], (L, KV_LORA_RANK, ATT_HEADS * (QK_NOPE_DIM + V_HEAD_DIM)), KV_LORA_RANK ** -0.5),
        "conv_w": nrm(ks[13], (L, CONV_KERNEL, CONV_WIDTH), CONV_KERNEL ** -0.5),
        "conv_b": nrm(ks[14], (L, CONV_WIDTH), 0.02),
        "conv_ln_g": gain(ks[15], (L, CONV_WIDTH)),
        "conv_ln_b": nrm(ks[16], (L, CONV_WIDTH), 0.02),
        "w_pw2": nrm(ks[17], (L, CONV_WIDTH, CONV_WIDTH), CONV_WIDTH ** -0.5),
        "sgu_ln_g": gain(ks[18], (L, SGU_WIDTH)),
        "sgu_ln_b": nrm(ks[19], (L, SGU_WIDTH), 0.02),
        "w_s": nrm(ks[20], (L, SGU_GROUPS, SGU_CHUNK, SGU_CHUNK), SGU_CHUNK ** -0.5),
        "b_s": gain(ks[21], (L, SGU_GROUPS, SGU_CHUNK)),
        "w_out": nrm(ks[22], (L, D_MIX, D_MODEL), D_MIX ** -0.5),
    }


def reference(x, c, positions, w_ada, b_ada, g_pre, g_post, w_in, q_norm_g, w_uq,
              kv_norm_g, w_ukv, conv_w, conv_b, conv_ln_g, conv_ln_b, w_pw2,
              sgu_ln_g, sgu_ln_b, w_s, b_s, w_out):
    for l in range(DEPTH):
        x = hybrid_layer(x, c, positions, w_ada[l], b_ada[l], g_pre[l], g_post[l], w_in[l],
                         q_norm_g[l], w_uq[l], kv_norm_g[l], w_ukv[l], conv_w[l], conv_b[l],
                         conv_ln_g[l], conv_ln_b[l], w_pw2[l], sgu_ln_g[l], sgu_ln_b[l],
                         w_s[l], b_s[l], w_out[l])
    return x
```

```python
import functools

import numpy as np
import jax
import jax.numpy as jnp
from jax import lax
from jax.experimental import pallas as pl
from jax.experimental.pallas import tpu as pltpu

F32 = jnp.float32
BF16 = jnp.bfloat16

NORM_EPS = 1e-6
ATT_HEADS = 4
QK_NOPE_DIM = 128
QK_ROPE_DIM = 64
V_HEAD_DIM = 128
HEAD_PAD = 256
ROPE_THETA = 10000.0
CONV_KERNEL = 31
CONV_HALO = 32
SGU_GROUPS = 4
SGU_CHUNK = 128
SQRT_HALF = float(np.sqrt(0.5).astype(np.float32))
ATT_SCALE = float((QK_NOPE_DIM + QK_ROPE_DIM) ** -0.5)
VMEM_LIMIT = 56 * 1024 * 1024


def _rms(x, g):
    return x * lax.rsqrt(jnp.mean(x * x, axis=-1, keepdims=True) + NORM_EPS) * g


def _layer_norm(x, g, b):
    mu = jnp.mean(x, axis=-1, keepdims=True)
    xc = x - mu
    var = jnp.mean(xc * xc, axis=-1, keepdims=True)
    return xc * lax.rsqrt(var + NORM_EPS) * g + b


def _sigmoid(x):
    return 1.0 / (1.0 + jnp.exp(-x))


def _silu(x):
    return x * _sigmoid(x)


def _gelu(x):
    return 0.5 * x * (1.0 + lax.erf(x * SQRT_HALF))


def _rope(t, tc, ts):
    lane = lax.broadcasted_iota(jnp.int32, t.shape, 1)
    first = lane < (QK_ROPE_DIM // 2)
    sw = jnp.where(first, pltpu.roll(t, 128 - QK_ROPE_DIM // 2, 1), pltpu.roll(t, QK_ROPE_DIM // 2, 1))
    return t * tc + jnp.where(first, -sw, sw) * ts


def _mod_kernel(c_ref, w_ref, b_ref, o_ref):
    c = c_ref[...]
    o_ref[...] = jnp.dot(_silu(c), w_ref[...], preferred_element_type=F32) + b_ref[...]


def _modulation(c_pad, w_ada, b_ada):
    L, D, D3 = w_ada.shape
    tn = 1024
    return pl.pallas_call(
        _mod_kernel,
        grid=(L, D3 // tn),
        in_specs=[
            pl.BlockSpec(c_pad.shape, lambda l, j: (0, 0)),
            pl.BlockSpec((None, D, tn), lambda l, j: (l, 0, j)),
            pl.BlockSpec((None, 1, tn), lambda l, j: (l, 0, j)),
        ],
        out_specs=pl.BlockSpec((None, c_pad.shape[0], tn), lambda l, j: (l, 0, j)),
        out_shape=jax.ShapeDtypeStruct((L, c_pad.shape[0], D3), F32),
        compiler_params=pltpu.CompilerParams(
            dimension_semantics=("arbitrary", "arbitrary"), vmem_limit_bytes=VMEM_LIMIT),
        name="adaln_mod",
    )(c_pad, w_ada, b_ada.reshape(L, 1, D3))


def _rope_table_kernel(pos_ref, invf_ref, cos_ref, sin_ref):
    ang = pos_ref[...].astype(F32) * invf_ref[...]
    cos_ref[...] = jnp.cos(ang)
    sin_ref[...] = jnp.sin(ang)


def _rope_tables(positions):
    B, S = positions.shape
    half = QK_ROPE_DIM // 2
    per_row = 128 // half
    n_rows = B * S // per_row
    inv_freq = ROPE_THETA ** (-jnp.arange(0, QK_ROPE_DIM, 2, dtype=F32) / QK_ROPE_DIM)
    pos_wide = jnp.repeat(positions.reshape(n_rows, per_row), half, axis=1)
    invf_wide = jnp.tile(inv_freq, per_row).reshape(1, 128)
    tr = min(n_rows, 1024)
    cos_c, sin_c = pl.pallas_call(
        _rope_table_kernel,
        grid=(n_rows // tr,),
        in_specs=[pl.BlockSpec((tr, 128), lambda i: (i, 0)), pl.BlockSpec((1, 128), lambda i: (0, 0))],
        out_specs=[pl.BlockSpec((tr, 128), lambda i: (i, 0))] * 2,
        out_shape=[jax.ShapeDtypeStruct((n_rows, 128), F32)] * 2,
        compiler_params=pltpu.CompilerParams(dimension_semantics=("arbitrary",)),
        name="rope_table",
    )(pos_wide, invf_wide)

    def widen(t):
        t = t.reshape(B, S, half)
        return jnp.concatenate([t, t, jnp.zeros((B, S, 128 - 2 * half), F32)], axis=-1)

    return widen(cos_c), widen(sin_c)


_C_QLAT, _C_KVLAT, _C_GATT, _C_CA, _C_CB, _C_GC, _C_SU, _C_SV, _C_GS, _C_KR, _C_END = (
    0, 256, 384, 896, 1152, 1408, 1664, 1920, 2176, 2432, 2560)


def _layer_in_kernel(x_ref, shift_ref, scale_ref, gpre_ref, win_ref, qg_ref, wuq_ref, kvg_ref, wukv_ref,
                     tc_ref, ts_ref, cw_ref, cb_ref, clg_ref, clb_ref, wpw_ref,
                     slg_ref, slb_ref, ws_ref, bs_ref,
                     q_out, k_out, v_out, g_out, hbuf):
    i = pl.program_id(1)
    tm = x_ref.shape[0]
    x = x_ref[...]
    h = _rms(x, gpre_ref[...]) * (1.0 + scale_ref[...]) + shift_ref[...]
    hb = h.astype(BF16)

    def proj(a, b):
        return jnp.dot(hb, win_ref[:, a:b], preferred_element_type=F32)

    tc = tc_ref[...]
    ts = ts_ref[...]

    qn = _rms(proj(_C_QLAT, _C_KVLAT), qg_ref[...]).astype(BF16)
    q = jnp.dot(qn, wuq_ref[...], preferred_element_type=F32)
    kvn = _rms(proj(_C_KVLAT, _C_GATT), kvg_ref[...]).astype(BF16)
    kv = jnp.dot(kvn, wukv_ref[...], preferred_element_type=F32)
    kr = _rope(proj(_C_KR, _C_END), tc, ts).astype(BF16)
    for hd in range(ATT_HEADS):
        o = hd * HEAD_PAD
        q_out[:, o:o + QK_NOPE_DIM] = (q[:, o:o + QK_NOPE_DIM] * ATT_SCALE).astype(BF16)
        q_out[:, o + QK_NOPE_DIM:o + HEAD_PAD] = (
            _rope(q[:, o + QK_NOPE_DIM:o + HEAD_PAD], tc, ts) * ATT_SCALE).astype(BF16)
        k_out[:, o:o + QK_NOPE_DIM] = kv[:, hd * QK_NOPE_DIM:(hd + 1) * QK_NOPE_DIM].astype(BF16)
        k_out[:, o + QK_NOPE_DIM:o + HEAD_PAD] = kr
    v_out[...] = kv[:, ATT_HEADS * QK_NOPE_DIM:].astype(BF16)

    g_out[:, 0:512] = _silu(proj(_C_GATT, _C_CA)).astype(BF16)

    glu = proj(_C_CA, _C_CB) * _sigmoid(proj(_C_CB, _C_GC))

    @pl.when(i == 0)
    def _():
        hbuf[0:CONV_HALO, :] = jnp.zeros((CONV_HALO, hbuf.shape[1]), F32)

    hbuf[CONV_HALO:CONV_HALO + tm, :] = glu
    acc = jnp.broadcast_to(cb_ref[...], glu.shape)
    base = CONV_HALO - (CONV_KERNEL - 1)
    for k in range(CONV_KERNEL):
        acc = acc + cw_ref[k:k + 1, :] * hbuf[pl.ds(base + k, tm), :]
    hbuf[0:CONV_HALO, :] = hbuf[tm:tm + CONV_HALO, :]
    hc = _silu(_layer_norm(acc, clg_ref[...], clb_ref[...])).astype(BF16)
    yc = jnp.dot(hc, wpw_ref[...], preferred_element_type=F32) * _silu(proj(_C_GC, _C_SU))
    g_out[:, 512:768] = yc.astype(BF16)

    u = _gelu(proj(_C_SU, _C_SV))
    vn = _layer_norm(_gelu(proj(_C_SV, _C_GS)), slg_ref[...], slb_ref[...]).astype(BF16)
    r = lax.broadcasted_iota(jnp.int32, (SGU_CHUNK, SGU_CHUNK), 0)
    c = lax.broadcasted_iota(jnp.int32, (SGU_CHUNK, SGU_CHUNK), 1)
    w_stack = jnp.concatenate(
        [jnp.where(c <= r, ws_ref[g], 0.0) for g in range(SGU_GROUPS)], axis=0).astype(BF16)
    gw = vn.shape[1] // SGU_GROUPS
    lane_grp = lax.broadcasted_iota(jnp.int32, (SGU_CHUNK, vn.shape[1]), 1) // gw
    svs = []
    for ck in range(tm // SGU_CHUNK):
        full = jnp.dot(w_stack, vn[ck * SGU_CHUNK:(ck + 1) * SGU_CHUNK, :], preferred_element_type=F32)
        sv = full[0:SGU_CHUNK]
        for g in range(1, SGU_GROUPS):
            sv = jnp.where(lane_grp == g, full[g * SGU_CHUNK:(g + 1) * SGU_CHUNK], sv)
        svs.append(sv + bs_ref[...])
    sv = jnp.concatenate(svs, axis=0)
    ys = u * sv * _silu(proj(_C_GS, _C_KR))
    g_out[:, 768:1024] = ys.astype(BF16)


def _layer_in(x, shift, scale, g_pre, w_in_p, q_norm_g, w_uq_p, kv_norm_g, w_ukv_p, tcos, tsin,
              conv_w_p, conv_b, conv_ln_g, conv_ln_b, w_pw2, sgu_ln_g, sgu_ln_b, w_s, bs_wide, tm):
    B, S, D = x.shape
    cw = conv_b.shape[-1]
    row = lambda a: a.reshape(1, -1)
    full = lambda a: pl.BlockSpec(a.shape, lambda b, i: (0,) * a.ndim)
    per_b = pl.BlockSpec((None, 1, D), lambda b, i: (b, 0, 0))
    seq = lambda w: pl.BlockSpec((None, tm, w), lambda b, i: (b, i, 0))
    args = [x, shift, scale, row(g_pre), w_in_p, row(q_norm_g), w_uq_p, row(kv_norm_g), w_ukv_p,
            tcos, tsin, conv_w_p, row(conv_b), row(conv_ln_g), row(conv_ln_b), w_pw2,
            row(sgu_ln_g), row(sgu_ln_b), w_s, bs_wide]
    in_specs = [seq(D), per_b, per_b] + [full(a) for a in args[3:9]] + [seq(128), seq(128)] + [
        full(a) for a in args[11:]]
    hp = ATT_HEADS * HEAD_PAD
    return pl.pallas_call(
        _layer_in_kernel,
        grid=(B, S // tm),
        in_specs=in_specs,
        out_specs=[seq(hp), seq(hp), seq(ATT_HEADS * V_HEAD_DIM), seq(D)],
        out_shape=[jax.ShapeDtypeStruct((B, S, hp), BF16), jax.ShapeDtypeStruct((B, S, hp), BF16),
                   jax.ShapeDtypeStruct((B, S, ATT_HEADS * V_HEAD_DIM), BF16),
                   jax.ShapeDtypeStruct((B, S, D), BF16)],
        scratch_shapes=[pltpu.VMEM((CONV_HALO + tm, cw), F32)],
        compiler_params=pltpu.CompilerParams(
            dimension_semantics=("arbitrary", "arbitrary"), vmem_limit_bytes=VMEM_LIMIT),
        name="layer_in",
    )(*args)


def _attn_kernel(q_ref, k_ref, v_ref, o_ref):
    i = pl.program_id(2)
    t = q_ref.shape[0]
    q = q_ref[...]

    def scores(j):
        k = k_ref[pl.ds(pl.multiple_of(j * t, t), t), :]
        return lax.dot_general(q, k, (((1,), (1,)), ((), ())), preferred_element_type=F32)

    def update(j, s, carry):
        m, l, acc = carry
        m_new = jnp.maximum(m, jnp.max(s, axis=-1, keepdims=True))
        alpha = jnp.exp(m - m_new)
        p = jnp.exp(s - m_new)
        l = alpha * l + jnp.sum(p, axis=-1, keepdims=True)
        v = v_ref[pl.ds(pl.multiple_of(j * t, t), t), :]
        acc = alpha * acc + jnp.dot(p.astype(BF16), v, preferred_element_type=F32)
        return m_new, l, acc

    init = (jnp.full((t, 1), -jnp.inf, F32), jnp.zeros((t, 1), F32), jnp.zeros((t, v_ref.shape[1]), F32))
    carry = lax.fori_loop(0, i, lambda j, cr: update(j, scores(j), cr), init)
    row = lax.broadcasted_iota(jnp.int32, (t, t), 0)
    col = lax.broadcasted_iota(jnp.int32, (t, t), 1)
    s = jnp.where(col <= row, scores(i), -jnp.inf)
    _, l, acc = update(i, s, carry)
    o_ref[...] = (acc / l).astype(o_ref.dtype)


def _attention(qp, kp, v, t):
    B, S, _ = qp.shape
    return pl.pallas_call(
        _attn_kernel,
        grid=(B, ATT_HEADS, S // t),
        in_specs=[
            pl.BlockSpec((None, t, HEAD_PAD), lambda b, h, i: (b, i, h)),
            pl.BlockSpec((None, S, HEAD_PAD), lambda b, h, i: (b, 0, h)),
            pl.BlockSpec((None, S, V_HEAD_DIM), lambda b, h, i: (b, 0, h)),
        ],
        out_specs=pl.BlockSpec((None, t, V_HEAD_DIM), lambda b, h, i: (b, i, h)),
        out_shape=jax.ShapeDtypeStruct((B, S, ATT_HEADS * V_HEAD_DIM), BF16),
        compiler_params=pltpu.CompilerParams(
            dimension_semantics=("arbitrary", "arbitrary", "arbitrary"), vmem_limit_bytes=VMEM_LIMIT),
        name="mla_attention",
    )(qp, kp, v)


def _layer_out_kernel(x_ref, y_ref, g_ref, gate_ref, wout_ref, gpost_ref, o_ref):
    aw = y_ref.shape[1]
    ya = (y_ref[...].astype(F32) * g_ref[:, 0:aw].astype(F32)).astype(BF16)
    y = (jnp.dot(ya, wout_ref[0:aw, :], preferred_element_type=F32)
         + jnp.dot(g_ref[:, aw:], wout_ref[aw:, :], preferred_element_type=F32))
    o_ref[...] = x_ref[...] + gate_ref[...] * _rms(y, gpost_ref[...])


def _layer_out(x, y_att, gated, gate, w_out_b, g_post, tm):
    B, S, D = x.shape
    seq = lambda w: pl.BlockSpec((None, tm, w), lambda b, i: (b, i, 0))
    return pl.pallas_call(
        _layer_out_kernel,
        grid=(B, S // tm),
        in_specs=[seq(D), seq(y_att.shape[-1]), seq(gated.shape[-1]),
                  pl.BlockSpec((None, 1, D), lambda b, i: (b, 0, 0)),
                  pl.BlockSpec(w_out_b.shape, lambda b, i: (0, 0)),
                  pl.BlockSpec((1, D), lambda b, i: (0, 0))],
        out_specs=seq(D),
        out_shape=jax.ShapeDtypeStruct((B, S, D), F32),
        compiler_params=pltpu.CompilerParams(
            dimension_semantics=("arbitrary", "arbitrary"), vmem_limit_bytes=VMEM_LIMIT),
        name="layer_out",
    )(x, y_att, gated, gate, w_out_b, g_post.reshape(1, D))


def _prep_w_in(w):
    D = w.shape[0]
    return jnp.concatenate(
        [w[:, :384], w[:, 448:], w[:, 384:448], jnp.zeros((D, 64), w.dtype)], axis=1).astype(BF16)


def _prep_w_uq(w):
    R = w.shape[0]
    w = w.reshape(R, ATT_HEADS, QK_NOPE_DIM + QK_ROPE_DIM)
    pad = jnp.zeros((R, ATT_HEADS, HEAD_PAD - QK_NOPE_DIM - QK_ROPE_DIM), w.dtype)
    return jnp.concatenate([w, pad], axis=-1).reshape(R, ATT_HEADS * HEAD_PAD).astype(BF16)


def _prep_w_ukv(w):
    R = w.shape[0]
    w = w.reshape(R, ATT_HEADS, QK_NOPE_DIM + V_HEAD_DIM)
    return jnp.concatenate(
        [w[:, :, :QK_NOPE_DIM].reshape(R, -1), w[:, :, QK_NOPE_DIM:].reshape(R, -1)], axis=1).astype(BF16)


def kernel(x, c, positions, w_ada, b_ada, g_pre, g_post, w_in, q_norm_g, w_uq, kv_norm_g, w_ukv, conv_w,
           conv_b, conv_ln_g, conv_ln_b, w_pw2, sgu_ln_g, sgu_ln_b, w_s, b_s, w_out):
    B, S, D = x.shape
    L = w_ada.shape[0]
    tm = min(S, 512)
    t_att = min(S, 512)

    c_pad = jnp.concatenate([c, jnp.zeros((8 - B % 8, D), c.dtype)], axis=0) if B % 8 else c
    mod = _modulation(c_pad, w_ada, b_ada)[:, :B]
    tcos, tsin = _rope_tables(positions)

    for l in range(L):
        shift = mod[l, :, 0:D].reshape(B, 1, D)
        scale = mod[l, :, D:2 * D].reshape(B, 1, D)
        gate = mod[l, :, 2 * D:3 * D].reshape(B, 1, D)
        conv_w_p = jnp.concatenate([conv_w[l], jnp.zeros((1, conv_w.shape[-1]), F32)], axis=0)
        bs_wide = jnp.repeat(b_s[l].T, w_pw2.shape[-1] // SGU_GROUPS, axis=1)
        qp, kp, v, gated = _layer_in(
            x, shift, scale, g_pre[l], _prep_w_in(w_in[l]), q_norm_g[l], _prep_w_uq(w_uq[l]),
            kv_norm_g[l], _prep_w_ukv(w_ukv[l]), tcos, tsin, conv_w_p, conv_b[l], conv_ln_g[l],
            conv_ln_b[l], w_pw2[l].astype(BF16), sgu_ln_g[l], sgu_ln_b[l], w_s[l], bs_wide, tm)
        y_att = _attention(qp, kp, v, t_att)
        x = _layer_out(x, y_att, gated, gate, w_out[l].astype(BF16), g_post[l], tm)
    return x
```

```python
import functools

import numpy as np
import jax
import jax.numpy as jnp
from jax import lax
from jax.experimental import pallas as pl
from jax.experimental.pallas import tpu as pltpu

F32 = jnp.float32
BF16 = jnp.bfloat16

NORM_EPS = 1e-6
ATT_HEADS = 4
QK_NOPE_DIM = 128
QK_ROPE_DIM = 64
V_HEAD_DIM = 128
HEAD_PAD = 256
ROPE_THETA = 10000.0
CONV_KERNEL = 31
CONV_HALO = 32
SGU_GROUPS = 4
SGU_CHUNK = 128
SQRT_HALF = float(np.sqrt(0.5).astype(np.float32))
ATT_SCALE = float((QK_NOPE_DIM + QK_ROPE_DIM) ** -0.5)
LOG2E = float(np.log2(np.e))
Q_SCALE = ATT_SCALE * LOG2E
VMEM_LIMIT = 56 * 1024 * 1024


def _rms(x, g):
    return x * lax.rsqrt(jnp.mean(x * x, axis=-1, keepdims=True) + NORM_EPS) * g


def _layer_norm(x, g, b):
    mu = jnp.mean(x, axis=-1, keepdims=True)
    xc = x - mu
    var = jnp.mean(xc * xc, axis=-1, keepdims=True)
    return xc * lax.rsqrt(var + NORM_EPS) * g + b


def _sigmoid(x):
    return 1.0 / (1.0 + jnp.exp(-x))


def _silu(x):
    return x * _sigmoid(x)


def _gelu(x):
    return 0.5 * x * (1.0 + lax.erf(x * SQRT_HALF))


def _rope(t, tc, ts):
    lane = lax.broadcasted_iota(jnp.int32, t.shape, 1)
    first = lane < (QK_ROPE_DIM // 2)
    sw = jnp.where(first, pltpu.roll(t, 128 - QK_ROPE_DIM // 2, 1), pltpu.roll(t, QK_ROPE_DIM // 2, 1))
    return t * tc + jnp.where(first, -sw, sw) * ts


def _mod_kernel(c_ref, w_ref, b_ref, o_ref):
    c = c_ref[...]
    o_ref[...] = jnp.dot(_silu(c), w_ref[...], preferred_element_type=F32) + b_ref[...]


def _modulation(c_pad, w_ada, b_ada):
    L, D, D3 = w_ada.shape
    tn = 1024
    return pl.pallas_call(
        _mod_kernel,
        grid=(L, D3 // tn),
        in_specs=[
            pl.BlockSpec(c_pad.shape, lambda l, j: (0, 0)),
            pl.BlockSpec((None, D, tn), lambda l, j: (l, 0, j)),
            pl.BlockSpec((None, 1, tn), lambda l, j: (l, 0, j)),
        ],
        out_specs=pl.BlockSpec((None, c_pad.shape[0], tn), lambda l, j: (l, 0, j)),
        out_shape=jax.ShapeDtypeStruct((L, c_pad.shape[0], D3), F32),
        compiler_params=pltpu.CompilerParams(
            dimension_semantics=("arbitrary", "arbitrary"), vmem_limit_bytes=VMEM_LIMIT),
        name="adaln_mod",
    )(c_pad, w_ada, b_ada.reshape(L, 1, D3))


def _rope_table_kernel(pos_ref, invf_ref, cos_ref, sin_ref):
    ang = pos_ref[...].astype(F32) * invf_ref[...]
    cos_ref[...] = jnp.cos(ang)
    sin_ref[...] = jnp.sin(ang)


def _rope_tables(positions):
    B, S = positions.shape
    half = QK_ROPE_DIM // 2
    per_row = 128 // half
    n_rows = B * S // per_row
    inv_freq = ROPE_THETA ** (-jnp.arange(0, QK_ROPE_DIM, 2, dtype=F32) / QK_ROPE_DIM)
    pos_wide = jnp.repeat(positions.reshape(n_rows, per_row), half, axis=1)
    invf_wide = jnp.tile(inv_freq, per_row).reshape(1, 128)
    tr = min(n_rows, 1024)
    cos_c, sin_c = pl.pallas_call(
        _rope_table_kernel,
        grid=(n_rows // tr,),
        in_specs=[pl.BlockSpec((tr, 128), lambda i: (i, 0)), pl.BlockSpec((1, 128), lambda i: (0, 0))],
        out_specs=[pl.BlockSpec((tr, 128), lambda i: (i, 0))] * 2,
        out_shape=[jax.ShapeDtypeStruct((n_rows, 128), F32)] * 2,
        compiler_params=pltpu.CompilerParams(dimension_semantics=("arbitrary",)),
        name="rope_table",
    )(pos_wide, invf_wide)

    def widen(t):
        t = t.reshape(B, S, half)
        return jnp.concatenate([t, t, jnp.zeros((B, S, 128 - 2 * half), F32)], axis=-1)

    return widen(cos_c), widen(sin_c)


_C_QLAT, _C_KVLAT, _C_GATT, _C_CA, _C_CB, _C_GC, _C_SU, _C_SV, _C_GS, _C_KR, _C_END = (
    0, 256, 384, 896, 1152, 1408, 1664, 1920, 2176, 2432, 2560)


def _layer_in_kernel(x_ref, shift_ref, scale_ref, gpre_ref, win_ref, qg_ref, wuq_ref, kvg_ref, wukv_ref,
                     tc_ref, ts_ref, cw_ref, cb_ref, clg_ref, clb_ref, wpw_ref,
                     slg_ref, slb_ref, ws_ref, bs_ref,
                     q_out, k_out, v_out, g_out, hbuf):
    i = pl.program_id(1)
    tm = x_ref.shape[0]
    x = x_ref[...]
    h = _rms(x, gpre_ref[...]) * (1.0 + scale_ref[...]) + shift_ref[...]
    hb = h.astype(BF16)

    def proj(a, b):
        return jnp.dot(hb, win_ref[:, a:b], preferred_element_type=F32)

    tc = tc_ref[...]
    ts = ts_ref[...]

    qn = _rms(proj(_C_QLAT, _C_KVLAT), qg_ref[...]).astype(BF16)
    q = jnp.dot(qn, wuq_ref[...], preferred_element_type=F32)
    kvn = _rms(proj(_C_KVLAT, _C_GATT), kvg_ref[...]).astype(BF16)
    kv = jnp.dot(kvn, wukv_ref[...], preferred_element_type=F32)
    kr = _rope(proj(_C_KR, _C_END), tc, ts).astype(BF16)
    for hd in range(ATT_HEADS):
        o = hd * HEAD_PAD
        q_out[:, o:o + QK_NOPE_DIM] = (q[:, o:o + QK_NOPE_DIM] * Q_SCALE).astype(BF16)
        q_out[:, o + QK_NOPE_DIM:o + HEAD_PAD] = (
            _rope(q[:, o + QK_NOPE_DIM:o + HEAD_PAD], tc, ts) * Q_SCALE).astype(BF16)
        k_out[:, o:o + QK_NOPE_DIM] = kv[:, hd * QK_NOPE_DIM:(hd + 1) * QK_NOPE_DIM].astype(BF16)
        k_out[:, o + QK_NOPE_DIM:o + HEAD_PAD] = kr
    v_out[...] = kv[:, ATT_HEADS * QK_NOPE_DIM:].astype(BF16)

    g_out[:, 0:512] = _silu(proj(_C_GATT, _C_CA)).astype(BF16)

    glu = proj(_C_CA, _C_CB) * _sigmoid(proj(_C_CB, _C_GC))

    @pl.when(i == 0)
    def _():
        hbuf[0:CONV_HALO, :] = jnp.zeros((CONV_HALO, hbuf.shape[1]), F32)

    hbuf[CONV_HALO:CONV_HALO + tm, :] = glu
    acc = jnp.broadcast_to(cb_ref[...], glu.shape)
    base = CONV_HALO - (CONV_KERNEL - 1)
    for k in range(CONV_KERNEL):
        acc = acc + cw_ref[k:k + 1, :] * hbuf[pl.ds(base + k, tm), :]
    hbuf[0:CONV_HALO, :] = hbuf[tm:tm + CONV_HALO, :]
    hc = _silu(_layer_norm(acc, clg_ref[...], clb_ref[...])).astype(BF16)
    yc = jnp.dot(hc, wpw_ref[...], preferred_element_type=F32) * _silu(proj(_C_GC, _C_SU))
    g_out[:, 512:768] = yc.astype(BF16)

    u = _gelu(proj(_C_SU, _C_SV))
    vn = _layer_norm(_gelu(proj(_C_SV, _C_GS)), slg_ref[...], slb_ref[...]).astype(BF16)
    r = lax.broadcasted_iota(jnp.int32, (SGU_CHUNK, SGU_CHUNK), 0)
    c = lax.broadcasted_iota(jnp.int32, (SGU_CHUNK, SGU_CHUNK), 1)
    w_stack = jnp.concatenate(
        [jnp.where(c <= r, ws_ref[g], 0.0) for g in range(SGU_GROUPS)], axis=0).astype(BF16)
    gw = vn.shape[1] // SGU_GROUPS
    lane_grp = lax.broadcasted_iota(jnp.int32, (SGU_CHUNK, vn.shape[1]), 1) // gw
    svs = []
    for ck in range(tm // SGU_CHUNK):
        full = jnp.dot(w_stack, vn[ck * SGU_CHUNK:(ck + 1) * SGU_CHUNK, :], preferred_element_type=F32)
        sv = full[0:SGU_CHUNK]
        for g in range(1, SGU_GROUPS):
            sv = jnp.where(lane_grp == g, full[g * SGU_CHUNK:(g + 1) * SGU_CHUNK], sv)
        svs.append(sv + bs_ref[...])
    sv = jnp.concatenate(svs, axis=0)
    ys = u * sv * _silu(proj(_C_GS, _C_KR))
    g_out[:, 768:1024] = ys.astype(BF16)


def _layer_in(x, shift, scale, g_pre, w_in_p, q_norm_g, w_uq_p, kv_norm_g, w_ukv_p, tcos, tsin,
              conv_w_p, conv_b, conv_ln_g, conv_ln_b, w_pw2, sgu_ln_g, sgu_ln_b, w_s, bs_wide, tm):
    B, S, D = x.shape
    cw = conv_b.shape[-1]
    row = lambda a: a.reshape(1, -1)
    full = lambda a: pl.BlockSpec(a.shape, lambda b, i: (0,) * a.ndim)
    per_b = pl.BlockSpec((None, 1, D), lambda b, i: (b, 0, 0))
    seq = lambda w: pl.BlockSpec((None, tm, w), lambda b, i: (b, i, 0))
    args = [x, shift, scale, row(g_pre), w_in_p, row(q_norm_g), w_uq_p, row(kv_norm_g), w_ukv_p,
            tcos, tsin, conv_w_p, row(conv_b), row(conv_ln_g), row(conv_ln_b), w_pw2,
            row(sgu_ln_g), row(sgu_ln_b), w_s, bs_wide]
    in_specs = [seq(D), per_b, per_b] + [full(a) for a in args[3:9]] + [seq(128), seq(128)] + [
        full(a) for a in args[11:]]
    hp = ATT_HEADS * HEAD_PAD
    return pl.pallas_call(
        _layer_in_kernel,
        grid=(B, S // tm),
        in_specs=in_specs,
        out_specs=[seq(hp), seq(hp), seq(ATT_HEADS * V_HEAD_DIM), seq(D)],
        out_shape=[jax.ShapeDtypeStruct((B, S, hp), BF16), jax.ShapeDtypeStruct((B, S, hp), BF16),
                   jax.ShapeDtypeStruct((B, S, ATT_HEADS * V_HEAD_DIM), BF16),
                   jax.ShapeDtypeStruct((B, S, D), BF16)],
        scratch_shapes=[pltpu.VMEM((CONV_HALO + tm, cw), F32)],
        compiler_params=pltpu.CompilerParams(
            dimension_semantics=("arbitrary", "arbitrary"), vmem_limit_bytes=VMEM_LIMIT),
        name="layer_in",
    )(*args)


ATT_ROW_BLOCK = 32


def _attn_kernel(q_ref, k_ref, v_ref, o_ref, s_a, s_b, p_a, p_b, al_a, al_b, m_ref, l_ref, acc_ref):
    i = pl.program_id(2)
    t = q_ref.shape[0]
    lanes = m_ref.shape[1]
    n_col = t // lanes
    buf_a = (s_a, p_a, al_a)
    buf_b = (s_b, p_b, al_b)

    def kv_rows(j):
        return pl.ds(pl.multiple_of(j * t, t), t)

    def score_tile(buf, j):
        buf[0][...] = lax.dot_general(q_ref[...], k_ref[kv_rows(j), :], (((1,), (1,)), ((), ())),
                                      preferred_element_type=F32)

    def softmax_tile(buf, masked=False):
        s_ref, p_ref, al_ref = buf
        for r0 in range(0, t, ATT_ROW_BLOCK):
            rows = slice(r0, r0 + ATT_ROW_BLOCK)
            s = [s_ref[rows, c * lanes:(c + 1) * lanes] for c in range(n_col)]
            if masked:
                row = r0 + lax.broadcasted_iota(jnp.int32, (ATT_ROW_BLOCK, lanes), 0)
                col = lax.broadcasted_iota(jnp.int32, (ATT_ROW_BLOCK, lanes), 1)
                s = [jnp.where(col + c * lanes <= row, s[c], -jnp.inf) for c in range(n_col)]
            mx = functools.reduce(jnp.maximum, s)
            m_old = m_ref[rows, :]
            m_new = jnp.maximum(m_old, jnp.max(mx, axis=-1, keepdims=True))
            p = [jnp.exp2(sc - m_new) for sc in s]
            alpha = jnp.exp2(m_old - m_new)
            l_ref[rows, :] = alpha * l_ref[rows, :] + jnp.sum(
                functools.reduce(jnp.add, p), axis=-1, keepdims=True)
            m_ref[rows, :] = m_new
            al_ref[rows, :] = alpha
            for c in range(n_col):
                p_ref[rows, c * lanes:(c + 1) * lanes] = p[c].astype(BF16)

    def value_tile(buf, j):
        _, p_ref, al_ref = buf
        acc_ref[...] = al_ref[...] * acc_ref[...] + jnp.dot(
            p_ref[...], v_ref[kv_rows(j), :], preferred_element_type=F32)

    def step(cur, other, j):
        score_tile(other, j + 1)
        softmax_tile(cur)
        value_tile(other, j - 1)

    m_ref[...] = jnp.full(m_ref.shape, -jnp.inf, F32)
    l_ref[...] = jnp.zeros(l_ref.shape, F32)
    acc_ref[...] = jnp.zeros(acc_ref.shape, F32)

    even = (i % 2) == 0

    @pl.when(i == 0)
    def _():
        score_tile(buf_a, 0)
        softmax_tile(buf_a, masked=True)

    @pl.when(jnp.logical_and(i > 0, even))
    def _():
        score_tile(buf_a, 0)
        score_tile(buf_b, 1)
        softmax_tile(buf_a)
        step(buf_b, buf_a, 1)

    @pl.when(jnp.logical_not(even))
    def _():
        score_tile(buf_b, 0)
        score_tile(buf_a, 1)
        softmax_tile(buf_b)

    first = jnp.where(even, 2, 1)

    def pair(jj, carry):
        j = first + 2 * jj
        step(buf_a, buf_b, j)
        step(buf_b, buf_a, j + 1)
        return carry

    lax.fori_loop(0, (i - 1) // 2, pair, 0)

    @pl.when(i > 0)
    def _():
        softmax_tile(buf_a, masked=True)
        value_tile(buf_b, i - 1)

    value_tile(buf_a, i)
    o_ref[...] = (acc_ref[...] / l_ref[...]).astype(o_ref.dtype)


def _attention(qp, kp, v, t):
    B, S, _ = qp.shape
    lanes = V_HEAD_DIM
    return pl.pallas_call(
        _attn_kernel,
        grid=(B, ATT_HEADS, S // t),
        in_specs=[
            pl.BlockSpec((None, t, HEAD_PAD), lambda b, h, i: (b, i, h)),
            pl.BlockSpec((None, S, HEAD_PAD), lambda b, h, i: (b, 0, h)),
            pl.BlockSpec((None, S, V_HEAD_DIM), lambda b, h, i: (b, 0, h)),
        ],
        out_specs=pl.BlockSpec((None, t, V_HEAD_DIM), lambda b, h, i: (b, i, h)),
        out_shape=jax.ShapeDtypeStruct((B, S, ATT_HEADS * V_HEAD_DIM), BF16),
        scratch_shapes=[pltpu.VMEM((t, t), F32), pltpu.VMEM((t, t), F32),
                        pltpu.VMEM((t, t), BF16), pltpu.VMEM((t, t), BF16),
                        pltpu.VMEM((t, lanes), F32), pltpu.VMEM((t, lanes), F32),
                        pltpu.VMEM((t, lanes), F32), pltpu.VMEM((t, lanes), F32),
                        pltpu.VMEM((t, V_HEAD_DIM), F32)],
        compiler_params=pltpu.CompilerParams(
            dimension_semantics=("arbitrary", "arbitrary", "arbitrary"), vmem_limit_bytes=VMEM_LIMIT),
        name="mla_attention",
    )(qp, kp, v)


def _layer_out_kernel(x_ref, y_ref, g_ref, gate_ref, wout_ref, gpost_ref, o_ref):
    aw = y_ref.shape[1]
    ya = (y_ref[...].astype(F32) * g_ref[:, 0:aw].astype(F32)).astype(BF16)
    y = (jnp.dot(ya, wout_ref[0:aw, :], preferred_element_type=F32)
         + jnp.dot(g_ref[:, aw:], wout_ref[aw:, :], preferred_element_type=F32))
    o_ref[...] = x_ref[...] + gate_ref[...] * _rms(y, gpost_ref[...])


def _layer_out(x, y_att, gated, gate, w_out_b, g_post, tm):
    B, S, D = x.shape
    seq = lambda w: pl.BlockSpec((None, tm, w), lambda b, i: (b, i, 0))
    return pl.pallas_call(
        _layer_out_kernel,
        grid=(B, S // tm),
        in_specs=[seq(D), seq(y_att.shape[-1]), seq(gated.shape[-1]),
                  pl.BlockSpec((None, 1, D), lambda b, i: (b, 0, 0)),
                  pl.BlockSpec(w_out_b.shape, lambda b, i: (0, 0)),
                  pl.BlockSpec((1, D), lambda b, i: (0, 0))],
        out_specs=seq(D),
        out_shape=jax.ShapeDtypeStruct((B, S, D), F32),
        compiler_params=pltpu.CompilerParams(
            dimension_semantics=("arbitrary", "arbitrary"), vmem_limit_bytes=VMEM_LIMIT),
        name="layer_out",
    )(x, y_att, gated, gate, w_out_b, g_post.reshape(1, D))


def _prep_w_in(w):
    D = w.shape[0]
    return jnp.concatenate(
        [w[:, :384], w[:, 448:], w[:, 384:448], jnp.zeros((D, 64), w.dtype)], axis=1).astype(BF16)


def _prep_w_uq(w):
    R = w.shape[0]
    w = w.reshape(R, ATT_HEADS, QK_NOPE_DIM + QK_ROPE_DIM)
    pad = jnp.zeros((R, ATT_HEADS, HEAD_PAD - QK_NOPE_DIM - QK_ROPE_DIM), w.dtype)
    return jnp.concatenate([w, pad], axis=-1).reshape(R, ATT_HEADS * HEAD_PAD).astype(BF16)


def _prep_w_ukv(w):
    R = w.shape[0]
    w = w.reshape(R, ATT_HEADS, QK_NOPE_DIM + V_HEAD_DIM)
    return jnp.concatenate(
        [w[:, :, :QK_NOPE_DIM].reshape(R, -1), w[:, :, QK_NOPE_DIM:].reshape(R, -1)], axis=1).astype(BF16)


def kernel(x, c, positions, w_ada, b_ada, g_pre, g_post, w_in, q_norm_g, w_uq, kv_norm_g, w_ukv, conv_w,
           conv_b, conv_ln_g, conv_ln_b, w_pw2, sgu_ln_g, sgu_ln_b, w_s, b_s, w_out):
    B, S, D = x.shape
    L = w_ada.shape[0]
    tm = min(S, 512)
    t_att = min(S, 512)

    c_pad = jnp.concatenate([c, jnp.zeros((8 - B % 8, D), c.dtype)], axis=0) if B % 8 else c
    mod = _modulation(c_pad, w_ada, b_ada)[:, :B]
    tcos, tsin = _rope_tables(positions)

    for l in range(L):
        shift = mod[l, :, 0:D].reshape(B, 1, D)
        scale = mod[l, :, D:2 * D].reshape(B, 1, D)
        gate = mod[l, :, 2 * D:3 * D].reshape(B, 1, D)
        conv_w_p = jnp.concatenate([conv_w[l], jnp.zeros((1, conv_w.shape[-1]), F32)], axis=0)
        bs_wide = jnp.repeat(b_s[l].T, w_pw2.shape[-1] // SGU_GROUPS, axis=1)
        qp, kp, v, gated = _layer_in(
            x, shift, scale, g_pre[l], _prep_w_in(w_in[l]), q_norm_g[l], _prep_w_uq(w_uq[l]),
            kv_norm_g[l], _prep_w_ukv(w_ukv[l]), tcos, tsin, conv_w_p, conv_b[l], conv_ln_g[l],
            conv_ln_b[l], w_pw2[l].astype(BF16), sgu_ln_g[l], sgu_ln_b[l], w_s[l], bs_wide, tm)
        y_att = _attention(qp, kp, v, t_att)
        x = _layer_out(x, y_att, gated, gate, w_out[l].astype(BF16), g_post[l], tm)
    return x
```

```python
import functools

import numpy as np
import jax
import jax.numpy as jnp
from jax import lax
from jax.experimental import pallas as pl
from jax.experimental.pallas import tpu as pltpu

F32 = jnp.float32
BF16 = jnp.bfloat16

NORM_EPS = 1e-6
ATT_HEADS = 4
QK_NOPE_DIM = 128
QK_ROPE_DIM = 64
V_HEAD_DIM = 128
HEAD_PAD = 256
ROPE_THETA = 10000.0
CONV_KERNEL = 31
CONV_HALO = 32
CONV_ROW_BLOCK = 64
SGU_GROUPS = 4
SGU_CHUNK = 128
SQRT_HALF = float(np.sqrt(0.5).astype(np.float32))
ATT_SCALE = float((QK_NOPE_DIM + QK_ROPE_DIM) ** -0.5)
LOG2E = float(np.log2(np.e))
Q_SCALE = ATT_SCALE * LOG2E
VMEM_LIMIT = 56 * 1024 * 1024


def _rms(x, g):
    return x * lax.rsqrt(jnp.mean(x * x, axis=-1, keepdims=True) + NORM_EPS) * g


def _layer_norm(x, g, b):
    mu = jnp.mean(x, axis=-1, keepdims=True)
    xc = x - mu
    var = jnp.mean(xc * xc, axis=-1, keepdims=True)
    return xc * lax.rsqrt(var + NORM_EPS) * g + b


def _sigmoid(x):
    return 1.0 / (1.0 + jnp.exp(-x))


def _silu(x):
    return x * _sigmoid(x)


def _gelu(x):
    return 0.5 * x * (1.0 + lax.erf(x * SQRT_HALF))


def _rope(t, tc, ts):
    lane = lax.broadcasted_iota(jnp.int32, t.shape, 1)
    first = lane < (QK_ROPE_DIM // 2)
    sw = jnp.where(first, pltpu.roll(t, 128 - QK_ROPE_DIM // 2, 1), pltpu.roll(t, QK_ROPE_DIM // 2, 1))
    return t * tc + jnp.where(first, -sw, sw) * ts


def _mod_kernel(c_ref, w_ref, b_ref, o_ref):
    c = c_ref[...]
    o_ref[...] = jnp.dot(_silu(c), w_ref[...], preferred_element_type=F32) + b_ref[...]


def _modulation(c_pad, w_ada, b_ada):
    L, D, D3 = w_ada.shape
    tn = 1024
    return pl.pallas_call(
        _mod_kernel,
        grid=(L, D3 // tn),
        in_specs=[
            pl.BlockSpec(c_pad.shape, lambda l, j: (0, 0)),
            pl.BlockSpec((None, D, tn), lambda l, j: (l, 0, j)),
            pl.BlockSpec((None, 1, tn), lambda l, j: (l, 0, j)),
        ],
        out_specs=pl.BlockSpec((None, c_pad.shape[0], tn), lambda l, j: (l, 0, j)),
        out_shape=jax.ShapeDtypeStruct((L, c_pad.shape[0], D3), F32),
        compiler_params=pltpu.CompilerParams(
            dimension_semantics=("arbitrary", "arbitrary"), vmem_limit_bytes=VMEM_LIMIT),
        name="adaln_mod",
    )(c_pad, w_ada, b_ada.reshape(L, 1, D3))


def _rope_table_kernel(pos_ref, invf_ref, cos_ref, sin_ref):
    ang = pos_ref[...].astype(F32) * invf_ref[...]
    cos_ref[...] = jnp.cos(ang)
    sin_ref[...] = jnp.sin(ang)


def _rope_tables(positions):
    B, S = positions.shape
    half = QK_ROPE_DIM // 2
    per_row = 128 // half
    n_rows = B * S // per_row
    inv_freq = ROPE_THETA ** (-jnp.arange(0, QK_ROPE_DIM, 2, dtype=F32) / QK_ROPE_DIM)
    pos_wide = jnp.repeat(positions.reshape(n_rows, per_row), half, axis=1)
    invf_wide = jnp.tile(inv_freq, per_row).reshape(1, 128)
    tr = min(n_rows, 1024)
    cos_c, sin_c = pl.pallas_call(
        _rope_table_kernel,
        grid=(n_rows // tr,),
        in_specs=[pl.BlockSpec((tr, 128), lambda i: (i, 0)), pl.BlockSpec((1, 128), lambda i: (0, 0))],
        out_specs=[pl.BlockSpec((tr, 128), lambda i: (i, 0))] * 2,
        out_shape=[jax.ShapeDtypeStruct((n_rows, 128), F32)] * 2,
        compiler_params=pltpu.CompilerParams(dimension_semantics=("arbitrary",)),
        name="rope_table",
    )(pos_wide, invf_wide)

    def widen(t):
        t = t.reshape(B, S, half)
        return jnp.concatenate([t, t, jnp.zeros((B, S, 128 - 2 * half), F32)], axis=-1)

    return widen(cos_c), widen(sin_c)


_C_QLAT, _C_KVLAT, _C_GATT, _C_CA, _C_CB, _C_GC, _C_SU, _C_SV, _C_GS, _C_KR, _C_END = (
    0, 256, 384, 896, 1152, 1408, 1664, 1920, 2176, 2432, 2560)


def _layer_in_kernel(x_ref, shift_ref, scale_ref, gpre_ref, win_ref, qg_ref, wuq_ref, kvg_ref, wukv_ref,
                     tc_ref, ts_ref, cw_ref, cb_ref, clg_ref, clb_ref, wpw_ref,
                     slg_ref, slb_ref, ws_ref, bs_ref,
                     q_out, k_out, v_out, g_out, hbuf, shifted):
    tm = x_ref.shape[0]

    @pl.when(pl.program_id(1) == 0)
    def _():
        hbuf[0:CONV_HALO, :] = jnp.zeros((CONV_HALO, hbuf.shape[1]), F32)

    x = x_ref[...]
    h = _rms(x, gpre_ref[...]) * (1.0 + scale_ref[...]) + shift_ref[...]
    hb = h.astype(BF16)

    def proj(a, b):
        return jnp.dot(hb, win_ref[:, a:b], preferred_element_type=F32)

    tc = tc_ref[...]
    ts = ts_ref[...]

    glu = proj(_C_CA, _C_CB) * _sigmoid(proj(_C_CB, _C_GC))
    hbuf[CONV_HALO:CONV_HALO + tm, :] = glu
    base = CONV_HALO - (CONV_KERNEL - 1)
    z_q = proj(_C_QLAT, _C_KVLAT)
    z_kv = proj(_C_KVLAT, _C_GATT)
    z_kr = proj(_C_KR, _C_END)
    for b in range(1, 8):
        shifted[b - 1, :, :] = hbuf[pl.ds(b, shifted.shape[1]), :]

    def conv_taps(r0):
        acc = jnp.broadcast_to(cb_ref[...], (CONV_ROW_BLOCK, glu.shape[1]))
        for k in range(CONV_KERNEL):
            a, b = divmod(base + k, 8)
            rows = pl.ds(r0 + 8 * a, CONV_ROW_BLOCK)
            src = hbuf[rows, :] if b == 0 else shifted[b - 1, rows, :]
            acc = acc + cw_ref[k:k + 1, :] * src
        return acc

    n_blk = tm // CONV_ROW_BLOCK
    taps = lambda part: [conv_taps(r * CONV_ROW_BLOCK)
                         for r in range(part * n_blk // 4, (part + 1) * n_blk // 4)]
    qn = _rms(z_q, qg_ref[...]).astype(BF16)
    kvn = _rms(z_kv, kvg_ref[...]).astype(BF16)
    q = jnp.dot(qn, wuq_ref[...], preferred_element_type=F32)
    kv = jnp.dot(kvn, wukv_ref[...], preferred_element_type=F32)
    blocks = taps(0) + taps(1)
    kr = _rope(z_kr, tc, ts).astype(BF16)
    for hd in range(ATT_HEADS):
        o = hd * HEAD_PAD
        q_out[:, o:o + QK_NOPE_DIM] = (q[:, o:o + QK_NOPE_DIM] * Q_SCALE).astype(BF16)
        q_out[:, o + QK_NOPE_DIM:o + HEAD_PAD] = (
            _rope(q[:, o + QK_NOPE_DIM:o + HEAD_PAD], tc, ts) * Q_SCALE).astype(BF16)
        k_out[:, o:o + QK_NOPE_DIM] = kv[:, hd * QK_NOPE_DIM:(hd + 1) * QK_NOPE_DIM].astype(BF16)
        k_out[:, o + QK_NOPE_DIM:o + HEAD_PAD] = kr
    v_out[...] = kv[:, ATT_HEADS * QK_NOPE_DIM:].astype(BF16)

    g_out[:, 0:512] = _silu(proj(_C_GATT, _C_CA)).astype(BF16)
    blocks += taps(2)

    vn = _layer_norm(_gelu(proj(_C_SV, _C_GS)), slg_ref[...], slb_ref[...]).astype(BF16)
    s_gc = _silu(proj(_C_GC, _C_SU))
    blocks += taps(3)
    hbuf[0:CONV_HALO, :] = hbuf[tm:tm + CONV_HALO, :]
    z_su = proj(_C_SU, _C_SV)
    z_gs = proj(_C_GS, _C_KR)

    hc = _silu(_layer_norm(jnp.concatenate(blocks, axis=0), clg_ref[...], clb_ref[...])).astype(BF16)
    yc = jnp.dot(hc, wpw_ref[...], preferred_element_type=F32) * s_gc
    g_out[:, 512:768] = yc.astype(BF16)

    r = lax.broadcasted_iota(jnp.int32, (SGU_CHUNK, SGU_CHUNK), 0)
    c = lax.broadcasted_iota(jnp.int32, (SGU_CHUNK, SGU_CHUNK), 1)
    w_stack = jnp.concatenate(
        [jnp.where(c <= r, ws_ref[g], 0.0) for g in range(SGU_GROUPS)], axis=0).astype(BF16)
    gw = vn.shape[1] // SGU_GROUPS
    lane_grp = lax.broadcasted_iota(jnp.int32, (SGU_CHUNK, vn.shape[1]), 1) // gw
    svs = []
    for ck in range(tm // SGU_CHUNK):
        full = jnp.dot(w_stack, vn[ck * SGU_CHUNK:(ck + 1) * SGU_CHUNK, :], preferred_element_type=F32)
        sv = full[0:SGU_CHUNK]
        for g in range(1, SGU_GROUPS):
            sv = jnp.where(lane_grp == g, full[g * SGU_CHUNK:(g + 1) * SGU_CHUNK], sv)
        svs.append(sv + bs_ref[...])
    sv = jnp.concatenate(svs, axis=0)
    ys = _gelu(z_su) * sv * _silu(z_gs)
    g_out[:, 768:1024] = ys.astype(BF16)


def _layer_in(x, shift, scale, g_pre, w_in_p, q_norm_g, w_uq_p, kv_norm_g, w_ukv_p, tcos, tsin,
              conv_w_p, conv_b, conv_ln_g, conv_ln_b, w_pw2, sgu_ln_g, sgu_ln_b, w_s, bs_wide, tm):
    B, S, D = x.shape
    cw = conv_b.shape[-1]
    row = lambda a: a.reshape(1, -1)
    full = lambda a: pl.BlockSpec(a.shape, lambda b, i: (0,) * a.ndim)
    per_b = pl.BlockSpec((None, 1, D), lambda b, i: (b, 0, 0))
    seq = lambda w: pl.BlockSpec((None, tm, w), lambda b, i: (b, i, 0))
    args = [x, shift, scale, row(g_pre), w_in_p, row(q_norm_g), w_uq_p, row(kv_norm_g), w_ukv_p,
            tcos, tsin, conv_w_p, row(conv_b), row(conv_ln_g), row(conv_ln_b), w_pw2,
            row(sgu_ln_g), row(sgu_ln_b), w_s, bs_wide]
    in_specs = [seq(D), per_b, per_b] + [full(a) for a in args[3:9]] + [seq(128), seq(128)] + [
        full(a) for a in args[11:]]
    hp = ATT_HEADS * HEAD_PAD
    return pl.pallas_call(
        _layer_in_kernel,
        grid=(B, S // tm),
        in_specs=in_specs,
        out_specs=[seq(hp), seq(hp), seq(ATT_HEADS * V_HEAD_DIM), seq(D)],
        out_shape=[jax.ShapeDtypeStruct((B, S, hp), BF16), jax.ShapeDtypeStruct((B, S, hp), BF16),
                   jax.ShapeDtypeStruct((B, S, ATT_HEADS * V_HEAD_DIM), BF16),
                   jax.ShapeDtypeStruct((B, S, D), BF16)],
        scratch_shapes=[pltpu.VMEM((CONV_HALO + tm, cw), F32),
                        pltpu.VMEM((7, CONV_HALO + tm - 8, cw), F32)],
        compiler_params=pltpu.CompilerParams(
            dimension_semantics=("arbitrary", "arbitrary"), vmem_limit_bytes=VMEM_LIMIT),
        name="layer_in",
    )(*args)


ATT_TILE = 512
ATT_ROW_BLOCK = 32


def _attn_kernel(q_ref, k_ref, v_ref, o_ref, s_a, s_b, p_a, p_b, al_a, al_b, m_st, l_st, acc_st):
    t = s_a.shape[0]
    nq = q_ref.shape[0] // t
    lanes = m_st.shape[2]
    n_col = t // lanes
    bufs = ((s_a, p_a, al_a), (s_b, p_b, al_b))

    def tile_rows(idx):
        if isinstance(idx, int):
            return pl.ds(idx * t, t)
        return pl.ds(pl.multiple_of(idx * t, t), t)

    def score_tile(buf, i, j):
        buf[0][...] = lax.dot_general(q_ref[tile_rows(i), :], k_ref[tile_rows(j), :],
                                      (((1,), (1,)), ((), ())), preferred_element_type=F32)

    def softmax_diag(buf, i):
        s_ref, p_ref, _ = buf
        for r0 in range(0, t, ATT_ROW_BLOCK):
            rows = pl.ds(r0, ATT_ROW_BLOCK)
            live = [c for c in range(n_col) if c * lanes <= r0 + ATT_ROW_BLOCK - 1]
            s = {}
            for c in live:
                sc = s_ref[rows, c * lanes:(c + 1) * lanes]
                if (c + 1) * lanes - 1 > r0:
                    row = r0 + lax.broadcasted_iota(jnp.int32, (ATT_ROW_BLOCK, lanes), 0)
                    col = c * lanes + lax.broadcasted_iota(jnp.int32, (ATT_ROW_BLOCK, lanes), 1)
                    sc = jnp.where(col <= row, sc, -jnp.inf)
                s[c] = sc
            m_new = jnp.max(functools.reduce(jnp.maximum, s.values()), axis=-1, keepdims=True)
            p = {c: jnp.exp2(s[c] - m_new) for c in live}
            l_new = jnp.sum(functools.reduce(jnp.add, p.values()), axis=-1, keepdims=True)
            m_st[i, rows, :] = jnp.broadcast_to(m_new, (ATT_ROW_BLOCK, lanes))
            l_st[i, rows, :] = jnp.broadcast_to(l_new, (ATT_ROW_BLOCK, lanes))
            for c in range(n_col):
                p_ref[rows, c * lanes:(c + 1) * lanes] = (
                    p[c].astype(BF16) if c in p else jnp.zeros((ATT_ROW_BLOCK, lanes), BF16))

    def softmax_lower(buf, i):
        s_ref, p_ref, al_ref = buf
        for r0 in range(0, t, ATT_ROW_BLOCK):
            rows = pl.ds(r0, ATT_ROW_BLOCK)
            s = [s_ref[rows, c * lanes:(c + 1) * lanes] for c in range(n_col)]
            m_old = m_st[i, rows, :]
            m_new = jnp.maximum(m_old, jnp.max(functools.reduce(jnp.maximum, s), axis=-1, keepdims=True))
            p = [jnp.exp2(sc - m_new) for sc in s]
            alpha = jnp.exp2(m_old - m_new)
            l_st[i, rows, :] = alpha * l_st[i, rows, :] + jnp.sum(
                functools.reduce(jnp.add, p), axis=-1, keepdims=True)
            m_st[i, rows, :] = m_new
            al_ref[rows, :] = alpha
            for c in range(n_col):
                p_ref[rows, c * lanes:(c + 1) * lanes] = p[c].astype(BF16)

    def value_first(buf, i, j):
        acc_st[i] = jnp.dot(buf[1][...], v_ref[tile_rows(j), :], preferred_element_type=F32)

    def value_accumulate(buf, i, j):
        acc_st[i] = buf[2][...] * acc_st[i] + jnp.dot(
            buf[1][...], v_ref[tile_rows(j), :], preferred_element_type=F32)

    if nq == 1:
        score_tile(bufs[0], 0, 0)
        softmax_diag(bufs[0], 0)
        value_first(bufs[0], 0, 0)
    else:
        u1 = 4 if nq % 4 == 0 else 2
        assert nq % u1 == 0
        p_b[...] = jnp.zeros(p_b.shape, BF16)
        score_tile(bufs[0], 0, 0)

        def diag_steps(it, carry):
            for d in range(u1):
                i = it * u1 + d
                cur, other = bufs[d % 2], bufs[(d + 1) % 2]
                nxt = jnp.minimum(i + 1, nq - 1)
                prv = jnp.maximum(i - 1, 0)
                score_tile(other, nxt, nxt)
                softmax_diag(cur, i)
                value_first(other, prv, prv)
            return carry

        lax.fori_loop(0, nq // u1, diag_steps, 0)
        value_first(bufs[(nq - 1) % 2], nq - 1, nq - 1)

    order = [(i, j) for i in range(1, nq) for j in range(i)]
    n_steps = len(order)

    def lower_step(n, cur_ij, prev_ij, next_ij):
        cur, other = bufs[n % 2], bufs[(n + 1) % 2]
        score_tile(other, *next_ij)
        softmax_lower(cur, cur_ij[0])
        value_accumulate(other, *prev_ij)

    if n_steps > 0:
        p_b[...] = jnp.zeros(p_b.shape, BF16)
        al_b[...] = jnp.ones(al_b.shape, F32)
        score_tile(bufs[0], 1, 0)

    if n_steps % 2 == 1:
        for n, ij in enumerate(order):
            lower_step(n, ij, order[max(n - 1, 0)], order[min(n + 1, n_steps - 1)])
        value_accumulate(bufs[(n_steps - 1) % 2], *order[-1])
    elif n_steps > 0:
        u2 = 8 if n_steps % 8 == 0 else 4 if n_steps % 4 == 0 else 2

        def advance(i, j):
            wrap = (j + 1) == i
            return jnp.where(wrap, i + 1, i), jnp.where(wrap, 0, j + 1)

        def lower_steps(it, carry):
            i, j, ip, jp = carry
            for d in range(u2):
                i_n, j_n = advance(i, j)
                lower_step(d, (i, j), (ip, jp), (jnp.minimum(i_n, nq - 1), j_n))
                i, j, ip, jp = i_n, j_n, i, j
            return i, j, ip, jp

        one = jnp.int32(1)
        zero = jnp.int32(0)
        _, _, ip, jp = lax.fori_loop(0, n_steps // u2, lower_steps, (one, zero, one, zero))
        value_accumulate(bufs[(n_steps - 1) % 2], ip, jp)

    def finish(i, carry):
        o_ref[tile_rows(i), :] = (acc_st[i] / l_st[i]).astype(o_ref.dtype)
        return carry

    lax.fori_loop(0, nq, finish, 0)


def _attention(qp, kp, v):
    B, S, _ = qp.shape
    t = min(S, ATT_TILE)
    nq = S // t
    lanes = V_HEAD_DIM
    head_block = lambda w: pl.BlockSpec((None, S, w), lambda b, h: (b, 0, h))
    return pl.pallas_call(
        _attn_kernel,
        grid=(B, ATT_HEADS),
        in_specs=[head_block(HEAD_PAD), head_block(HEAD_PAD), head_block(V_HEAD_DIM)],
        out_specs=head_block(V_HEAD_DIM),
        out_shape=jax.ShapeDtypeStruct((B, S, ATT_HEADS * V_HEAD_DIM), BF16),
        scratch_shapes=[pltpu.VMEM((t, t), F32), pltpu.VMEM((t, t), F32),
                        pltpu.VMEM((t, t), BF16), pltpu.VMEM((t, t), BF16),
                        pltpu.VMEM((t, lanes), F32), pltpu.VMEM((t, lanes), F32),
                        pltpu.VMEM((nq, t, lanes), F32),
                        pltpu.VMEM((nq, t, lanes), F32),
                        pltpu.VMEM((nq, t, V_HEAD_DIM), F32)],
        compiler_params=pltpu.CompilerParams(
            dimension_semantics=("arbitrary", "arbitrary"), vmem_limit_bytes=VMEM_LIMIT),
        name="mla_attention",
    )(qp, kp, v)


def _layer_out_kernel(x_ref, y_ref, g_ref, gate_ref, wout_ref, gpost_ref, o_ref):
    aw = y_ref.shape[1]
    ya = (y_ref[...].astype(F32) * g_ref[:, 0:aw].astype(F32)).astype(BF16)
    y = (jnp.dot(ya, wout_ref[0:aw, :], preferred_element_type=F32)
         + jnp.dot(g_ref[:, aw:], wout_ref[aw:, :], preferred_element_type=F32))
    o_ref[...] = x_ref[...] + gate_ref[...] * _rms(y, gpost_ref[...])


def _layer_out(x, y_att, gated, gate, w_out_b, g_post, tm):
    B, S, D = x.shape
    seq = lambda w: pl.BlockSpec((None, tm, w), lambda b, i: (b, i, 0))
    return pl.pallas_call(
        _layer_out_kernel,
        grid=(B, S // tm),
        in_specs=[seq(D), seq(y_att.shape[-1]), seq(gated.shape[-1]),
                  pl.BlockSpec((None, 1, D), lambda b, i: (b, 0, 0)),
                  pl.BlockSpec(w_out_b.shape, lambda b, i: (0, 0)),
                  pl.BlockSpec((1, D), lambda b, i: (0, 0))],
        out_specs=seq(D),
        out_shape=jax.ShapeDtypeStruct((B, S, D), F32),
        compiler_params=pltpu.CompilerParams(
            dimension_semantics=("arbitrary", "arbitrary"), vmem_limit_bytes=VMEM_LIMIT),
        name="layer_out",
    )(x, y_att, gated, gate, w_out_b, g_post.reshape(1, D))


def _prep_w_in(w):
    D = w.shape[0]
    return jnp.concatenate(
        [w[:, :384], w[:, 448:], w[:, 384:448], jnp.zeros((D, 64), w.dtype)], axis=1).astype(BF16)


def _prep_w_uq(w):
    R = w.shape[0]
    w = w.reshape(R, ATT_HEADS, QK_NOPE_DIM + QK_ROPE_DIM)
    pad = jnp.zeros((R, ATT_HEADS, HEAD_PAD - QK_NOPE_DIM - QK_ROPE_DIM), w.dtype)
    return jnp.concatenate([w, pad], axis=-1).reshape(R, ATT_HEADS * HEAD_PAD).astype(BF16)


def _prep_w_ukv(w):
    R = w.shape[0]
    w = w.reshape(R, ATT_HEADS, QK_NOPE_DIM + V_HEAD_DIM)
    return jnp.concatenate(
        [w[:, :, :QK_NOPE_DIM].reshape(R, -1), w[:, :, QK_NOPE_DIM:].reshape(R, -1)], axis=1).astype(BF16)


def kernel(x, c, positions, w_ada, b_ada, g_pre, g_post, w_in, q_norm_g, w_uq, kv_norm_g, w_ukv, conv_w,
           conv_b, conv_ln_g, conv_ln_b, w_pw2, sgu_ln_g, sgu_ln_b, w_s, b_s, w_out):
    B, S, D = x.shape
    L = w_ada.shape[0]
    tm = min(S, 512)

    c_pad = jnp.concatenate([c, jnp.zeros((8 - B % 8, D), c.dtype)], axis=0) if B % 8 else c
    mod = _modulation(c_pad, w_ada, b_ada)[:, :B]
    tcos, tsin = _rope_tables(positions)

    for l in range(L):
        shift = mod[l, :, 0:D].reshape(B, 1, D)
        scale = mod[l, :, D:2 * D].reshape(B, 1, D)
        gate = mod[l, :, 2 * D:3 * D].reshape(B, 1, D)
        conv_w_p = jnp.concatenate([conv_w[l], jnp.zeros((1, conv_w.shape[-1]), F32)], axis=0)
        bs_wide = jnp.repeat(b_s[l].T, w_pw2.shape[-1] // SGU_GROUPS, axis=1)
        qp, kp, v, gated = _layer_in(
            x, shift, scale, g_pre[l], _prep_w_in(w_in[l]), q_norm_g[l], _prep_w_uq(w_uq[l]),
            kv_norm_g[l], _prep_w_ukv(w_ukv[l]), tcos, tsin, conv_w_p, conv_b[l], conv_ln_g[l],
            conv_ln_b[l], w_pw2[l].astype(BF16), sgu_ln_g[l], sgu_ln_b[l], w_s[l], bs_wide, tm)
        y_att = _attention(qp, kp, v)
        x = _layer_out(x, y_att, gated, gate, w_out[l].astype(BF16), g_post[l], tm)
    return x
```

```python
import functools

import numpy as np
import jax
import jax.numpy as jnp
from jax import lax
from jax.experimental import pallas as pl
from jax.experimental.pallas import tpu as pltpu

F32 = jnp.float32
BF16 = jnp.bfloat16

NORM_EPS = 1e-6
ATT_HEADS = 4
QK_NOPE_DIM = 128
QK_ROPE_DIM = 64
V_HEAD_DIM = 128
HEAD_PAD = 256
ROPE_THETA = 10000.0
CONV_KERNEL = 31
CONV_HALO = 32
CONV_ROW_BLOCK = 64
LAYER_IN_ROW_GROUPS = 2
SGU_GROUPS = 4
SGU_CHUNK = 128
SQRT_HALF = float(np.sqrt(0.5).astype(np.float32))
ATT_SCALE = float((QK_NOPE_DIM + QK_ROPE_DIM) ** -0.5)
LOG2E = float(np.log2(np.e))
Q_SCALE = ATT_SCALE * LOG2E
VMEM_LIMIT = 56 * 1024 * 1024


def _rms(x, g):
    return x * lax.rsqrt(jnp.mean(x * x, axis=-1, keepdims=True) + NORM_EPS) * g


def _layer_norm(x, g, b):
    mu = jnp.mean(x, axis=-1, keepdims=True)
    xc = x - mu
    var = jnp.mean(xc * xc, axis=-1, keepdims=True)
    return xc * lax.rsqrt(var + NORM_EPS) * g + b


def _sigmoid(x):
    return 1.0 / (1.0 + jnp.exp(-x))


def _silu(x):
    return x * _sigmoid(x)


def _gelu(x):
    return 0.5 * x * (1.0 + lax.erf(x * SQRT_HALF))


def _rope(t, tc, ts):
    lane = lax.broadcasted_iota(jnp.int32, t.shape, 1)
    first = lane < (QK_ROPE_DIM // 2)
    sw = jnp.where(first, pltpu.roll(t, 128 - QK_ROPE_DIM // 2, 1), pltpu.roll(t, QK_ROPE_DIM // 2, 1))
    return t * tc + jnp.where(first, -sw, sw) * ts


def _mod_kernel(c_ref, w_ref, b_ref, o_ref):
    c = c_ref[...]
    o_ref[...] = jnp.dot(_silu(c), w_ref[...], preferred_element_type=F32) + b_ref[...]


def _modulation(c_pad, w_ada, b_ada):
    L, D, D3 = w_ada.shape
    tn = 1024
    return pl.pallas_call(
        _mod_kernel,
        grid=(L, D3 // tn),
        in_specs=[
            pl.BlockSpec(c_pad.shape, lambda l, j: (0, 0)),
            pl.BlockSpec((None, D, tn), lambda l, j: (l, 0, j)),
            pl.BlockSpec((None, 1, tn), lambda l, j: (l, 0, j)),
        ],
        out_specs=pl.BlockSpec((None, c_pad.shape[0], tn), lambda l, j: (l, 0, j)),
        out_shape=jax.ShapeDtypeStruct((L, c_pad.shape[0], D3), F32),
        compiler_params=pltpu.CompilerParams(
            dimension_semantics=("arbitrary", "arbitrary"), vmem_limit_bytes=VMEM_LIMIT),
        name="adaln_mod",
    )(c_pad, w_ada, b_ada.reshape(L, 1, D3))


def _rope_table_kernel(pos_ref, invf_ref, cos_ref, sin_ref):
    ang = pos_ref[...].astype(F32) * invf_ref[...]
    cos_ref[...] = jnp.cos(ang)
    sin_ref[...] = jnp.sin(ang)


def _rope_tables(positions):
    B, S = positions.shape
    half = QK_ROPE_DIM // 2
    per_row = 128 // half
    n_rows = B * S // per_row
    inv_freq = ROPE_THETA ** (-jnp.arange(0, QK_ROPE_DIM, 2, dtype=F32) / QK_ROPE_DIM)
    pos_wide = jnp.broadcast_to(
        positions.reshape(n_rows, per_row, 1), (n_rows, per_row, half)).reshape(n_rows, 128)
    invf_wide = jnp.tile(inv_freq, per_row).reshape(1, 128)
    tr = min(n_rows, 1024)
    cos_c, sin_c = pl.pallas_call(
        _rope_table_kernel,
        grid=(n_rows // tr,),
        in_specs=[pl.BlockSpec((tr, 128), lambda i: (i, 0)), pl.BlockSpec((1, 128), lambda i: (0, 0))],
        out_specs=[pl.BlockSpec((tr, 128), lambda i: (i, 0))] * 2,
        out_shape=[jax.ShapeDtypeStruct((n_rows, 128), F32)] * 2,
        compiler_params=pltpu.CompilerParams(dimension_semantics=("arbitrary",)),
        name="rope_table",
    )(pos_wide, invf_wide)

    def widen(t):
        t = t.reshape(B, S, half)
        return jnp.concatenate([t, t, jnp.zeros((B, S, 128 - 2 * half), F32)], axis=-1)

    return widen(cos_c), widen(sin_c)


_C_QLAT, _C_KVLAT, _C_GATT, _C_CA, _C_CB, _C_GC, _C_SU, _C_SV, _C_GS, _C_KR, _C_END = (
    0, 256, 384, 896, 1152, 1408, 1664, 1920, 2176, 2432, 2560)


def _layer_in_kernel(x_ref, shift_ref, scale_ref, gpre_ref, win_ref, qg_ref, wuq_ref, kvg_ref, wukv_ref,
                     tc_ref, ts_ref, cw_ref, cb_ref, clg_ref, clb_ref, wpw_ref,
                     slg_ref, slb_ref, ws_ref, bs_ref,
                     q_out, k_out, v_out, g_out, hbuf, shifted):
    tm = x_ref.shape[0]
    rg = tm // LAYER_IN_ROW_GROUPS
    base = CONV_HALO - (CONV_KERNEL - 1)

    @pl.when(pl.program_id(1) == 0)
    def _():
        hbuf[0:CONV_HALO, :] = jnp.zeros((CONV_HALO, hbuf.shape[1]), F32)

    r = lax.broadcasted_iota(jnp.int32, (SGU_CHUNK, SGU_CHUNK), 0)
    c = lax.broadcasted_iota(jnp.int32, (SGU_CHUNK, SGU_CHUNK), 1)
    w_stack = jnp.concatenate(
        [jnp.where(c <= r, ws_ref[g], 0.0) for g in range(SGU_GROUPS)], axis=0).astype(BF16)
    lane_grp = lax.broadcasted_iota(jnp.int32, (SGU_CHUNK, hbuf.shape[1]), 1) // (hbuf.shape[1] // SGU_GROUPS)

    def normalise(g):
        x = x_ref[pl.ds(g * rg, rg), :]
        h = _rms(x, gpre_ref[...]) * (1.0 + scale_ref[...]) + shift_ref[...]
        return h.astype(BF16)

    def project(hb):
        cols = {"q": (_C_QLAT, _C_KVLAT), "kv": (_C_KVLAT, _C_GATT), "gatt": (_C_GATT, _C_CA),
                "ca": (_C_CA, _C_CB), "cb": (_C_CB, _C_GC), "gc": (_C_GC, _C_SU), "su": (_C_SU, _C_SV),
                "sv": (_C_SV, _C_GS), "gs": (_C_GS, _C_KR), "kr": (_C_KR, _C_END)}
        return {n: jnp.dot(hb, win_ref[:, a:b], preferred_element_type=F32) for n, (a, b) in cols.items()}

    def conv_taps(r0):
        acc = jnp.broadcast_to(cb_ref[...], (CONV_ROW_BLOCK, hbuf.shape[1]))
        for k in range(CONV_KERNEL):
            a, b = divmod(base + k, 8)
            rows = pl.ds(r0 + 8 * a, CONV_ROW_BLOCK)
            src = hbuf[rows, :] if b == 0 else shifted[b - 1, rows, :]
            acc = acc + cw_ref[k:k + 1, :] * src
        return acc

    def mix(g, z):
        rows = pl.ds(g * rg, rg)
        tc = tc_ref[rows, :]
        ts = ts_ref[rows, :]

        hbuf[pl.ds(CONV_HALO + g * rg, rg), :] = z["ca"] * _sigmoid(z["cb"])
        lo = g * rg + (CONV_HALO - 8 if g else 0)
        hi = (g + 1) * rg + CONV_HALO - 8
        for b in range(1, 8):
            shifted[b - 1, pl.ds(lo, hi - lo), :] = hbuf[pl.ds(lo + b, hi - lo), :]
        conv = jnp.concatenate(
            [conv_taps(g * rg + r0) for r0 in range(0, rg, CONV_ROW_BLOCK)], axis=0)

        qn = _rms(z["q"], qg_ref[...]).astype(BF16)
        kvn = _rms(z["kv"], kvg_ref[...]).astype(BF16)
        q = jnp.dot(qn, wuq_ref[...], preferred_element_type=F32)
        kv = jnp.dot(kvn, wukv_ref[...], preferred_element_type=F32)
        kr = _rope(z["kr"], tc, ts).astype(BF16)
        for hd in range(ATT_HEADS):
            o = hd * HEAD_PAD
            q_out[rows, o:o + QK_NOPE_DIM] = (q[:, o:o + QK_NOPE_DIM] * Q_SCALE).astype(BF16)
            q_out[rows, o + QK_NOPE_DIM:o + HEAD_PAD] = (
                _rope(q[:, o + QK_NOPE_DIM:o + HEAD_PAD], tc, ts) * Q_SCALE).astype(BF16)
            k_out[rows, o:o + QK_NOPE_DIM] = kv[:, hd * QK_NOPE_DIM:(hd + 1) * QK_NOPE_DIM].astype(BF16)
            k_out[rows, o + QK_NOPE_DIM:o + HEAD_PAD] = kr
        v_out[rows, :] = kv[:, ATT_HEADS * QK_NOPE_DIM:].astype(BF16)

        g_out[rows, 0:512] = _silu(z["gatt"]).astype(BF16)

        hc = _silu(_layer_norm(conv, clg_ref[...], clb_ref[...])).astype(BF16)
        yc = jnp.dot(hc, wpw_ref[...], preferred_element_type=F32) * _silu(z["gc"])
        g_out[rows, 512:768] = yc.astype(BF16)

        vn = _layer_norm(_gelu(z["sv"]), slg_ref[...], slb_ref[...]).astype(BF16)
        svs = []
        for ck in range(rg // SGU_CHUNK):
            full = jnp.dot(w_stack, vn[ck * SGU_CHUNK:(ck + 1) * SGU_CHUNK, :], preferred_element_type=F32)
            sv = full[0:SGU_CHUNK]
            for grp in range(1, SGU_GROUPS):
                sv = jnp.where(lane_grp == grp, full[grp * SGU_CHUNK:(grp + 1) * SGU_CHUNK], sv)
            svs.append(sv + bs_ref[...])
        ys = _gelu(z["su"]) * jnp.concatenate(svs, axis=0) * _silu(z["gs"])
        g_out[rows, 768:1024] = ys.astype(BF16)

    hbs = [normalise(g) for g in range(LAYER_IN_ROW_GROUPS)]
    z = project(hbs[0])
    for g in range(LAYER_IN_ROW_GROUPS):
        z_next = project(hbs[g + 1]) if g + 1 < LAYER_IN_ROW_GROUPS else None
        mix(g, z)
        z = z_next
    hbuf[0:CONV_HALO, :] = hbuf[tm:tm + CONV_HALO, :]


def _mod_spec(l, part, D):
    return pl.BlockSpec((None, None, None, 1, D), lambda b, i: (l, b, part, 0, 0))


def _layer_spec(l, a):
    return pl.BlockSpec((None,) + a.shape[1:], lambda b, i: (l,) + (0,) * (a.ndim - 1))


def _layer_in(l, x, mod, tcos, tsin, params, tm):
    B, S, D = x.shape
    cw = params[7].shape[-1]
    seq = lambda w: pl.BlockSpec((None, tm, w), lambda b, i: (b, i, 0))
    args = [x, mod, mod] + list(params[:6]) + [tcos, tsin] + list(params[6:])
    in_specs = ([seq(D), _mod_spec(l, 0, D), _mod_spec(l, 1, D)] + [_layer_spec(l, a) for a in params[:6]]
                + [seq(128), seq(128)] + [_layer_spec(l, a) for a in params[6:]])
    hp = ATT_HEADS * HEAD_PAD
    return pl.pallas_call(
        _layer_in_kernel,
        grid=(B, S // tm),
        in_specs=in_specs,
        out_specs=[seq(hp), seq(hp), seq(ATT_HEADS * V_HEAD_DIM), seq(D)],
        out_shape=[jax.ShapeDtypeStruct((B, S, hp), BF16), jax.ShapeDtypeStruct((B, S, hp), BF16),
                   jax.ShapeDtypeStruct((B, S, ATT_HEADS * V_HEAD_DIM), BF16),
                   jax.ShapeDtypeStruct((B, S, D), BF16)],
        scratch_shapes=[pltpu.VMEM((CONV_HALO + tm, cw), F32),
                        pltpu.VMEM((7, CONV_HALO + tm - 8, cw), F32)],
        compiler_params=pltpu.CompilerParams(
            dimension_semantics=("arbitrary", "arbitrary"), vmem_limit_bytes=VMEM_LIMIT),
        name="layer_in",
    )(*args)


ATT_TILE = 512
ATT_ROW_BLOCK = 16


def _attn_kernel(q_ref, k_ref, v_ref, o_ref, s_a, s_b, s_c, p_a, p_b, p_c, al_a, al_b, al_c, m_st, l_st, acc_st):
    t = s_a.shape[0]
    nq = q_ref.shape[0] // t
    lanes = m_st.shape[2]
    n_col = t // lanes
    s_ring = (s_a, s_b, s_c)
    p_ring = ((p_a, al_a), (p_b, al_b), (p_c, al_c))

    def tile_rows(idx):
        if isinstance(idx, int):
            return pl.ds(idx * t, t)
        return pl.ds(pl.multiple_of(idx * t, t), t)

    def score_tile(s_ref, i, j):
        s_ref[...] = lax.dot_general(q_ref[tile_rows(i), :], k_ref[tile_rows(j), :],
                                     (((1,), (1,)), ((), ())), preferred_element_type=F32)

    def softmax_diag(s_ref, p_buf, i):
        p_ref, _ = p_buf
        for r0 in range(0, t, ATT_ROW_BLOCK):
            rows = pl.ds(r0, ATT_ROW_BLOCK)
            live = [c for c in range(n_col) if c * lanes <= r0 + ATT_ROW_BLOCK - 1]
            s = {}
            for c in live:
                sc = s_ref[rows, c * lanes:(c + 1) * lanes]
                if (c + 1) * lanes - 1 > r0:
                    row = r0 + lax.broadcasted_iota(jnp.int32, (ATT_ROW_BLOCK, lanes), 0)
                    col = c * lanes + lax.broadcasted_iota(jnp.int32, (ATT_ROW_BLOCK, lanes), 1)
                    sc = jnp.where(col <= row, sc, -jnp.inf)
                s[c] = sc
            m_new = jnp.max(functools.reduce(jnp.maximum, s.values()), axis=-1, keepdims=True)
            p = {c: jnp.exp2(s[c] - m_new) for c in live}
            l_new = jnp.sum(functools.reduce(jnp.add, p.values()), axis=-1, keepdims=True)
            m_st[i, rows, :] = jnp.broadcast_to(m_new, (ATT_ROW_BLOCK, lanes))
            l_st[i, rows, :] = jnp.broadcast_to(l_new, (ATT_ROW_BLOCK, lanes))
            for c in range(n_col):
                p_ref[rows, c * lanes:(c + 1) * lanes] = (
                    p[c].astype(BF16) if c in p else jnp.zeros((ATT_ROW_BLOCK, lanes), BF16))

    def softmax_lower(s_ref, p_buf, i):
        p_ref, al_ref = p_buf
        for r0 in range(0, t, ATT_ROW_BLOCK):
            rows = pl.ds(r0, ATT_ROW_BLOCK)
            s = [s_ref[rows, c * lanes:(c + 1) * lanes] for c in range(n_col)]
            m_old = m_st[i, rows, :]
            m_new = jnp.maximum(m_old, jnp.max(functools.reduce(jnp.maximum, s), axis=-1, keepdims=True))
            p = [jnp.exp2(sc - m_new) for sc in s]
            alpha = jnp.exp2(m_old - m_new)
            l_st[i, rows, :] = alpha * l_st[i, rows, :] + jnp.sum(
                functools.reduce(jnp.add, p), axis=-1, keepdims=True)
            m_st[i, rows, :] = m_new
            al_ref[rows, :] = alpha
            for c in range(n_col):
                p_ref[rows, c * lanes:(c + 1) * lanes] = p[c].astype(BF16)

    def value_first(p_buf, i, j):
        acc_st[i] = jnp.dot(p_buf[0][...], v_ref[tile_rows(j), :], preferred_element_type=F32)

    def value_accumulate(p_buf, i, j):
        acc_st[i] = p_buf[1][...] * acc_st[i] + jnp.dot(
            p_buf[0][...], v_ref[tile_rows(j), :], preferred_element_type=F32)

    if nq == 1:
        score_tile(s_ring[0], 0, 0)
        softmax_diag(s_ring[0], p_ring[0], 0)
        value_first(p_ring[0], 0, 0)
    else:
        u1 = 4 if nq % 4 == 0 else 2
        assert nq % u1 == 0
        p_b[...] = jnp.zeros(p_b.shape, BF16)
        score_tile(s_ring[0], 0, 0)

        def diag_steps(it, carry):
            for d in range(u1):
                i = it * u1 + d
                cur, other = d % 2, (d + 1) % 2
                nxt = jnp.minimum(i + 1, nq - 1)
                prv = jnp.maximum(i - 1, 0)
                score_tile(s_ring[other], nxt, nxt)
                softmax_diag(s_ring[cur], p_ring[cur], i)
                value_first(p_ring[other], prv, prv)
            return carry

        lax.fori_loop(0, nq // u1, diag_steps, 0)
        value_first(p_ring[(nq - 1) % 2], nq - 1, nq - 1)

    order = [(i, j) for i in range(1, nq) for j in range(i)]
    n_steps = len(order)

    def lower_step(n, cur_ij, prev_ij, next_ij):
        score_tile(s_ring[(n + 1) % 3], *next_ij)
        softmax_lower(s_ring[n % 3], p_ring[n % 3], cur_ij[0])
        value_accumulate(p_ring[(n - 1) % 3], *prev_ij)

    if n_steps > 0:
        p_c[...] = jnp.zeros(p_c.shape, BF16)
        al_c[...] = jnp.ones(al_c.shape, F32)
        score_tile(s_ring[0], 1, 0)

    unrolls = [u for u in (12, 6, 3) if n_steps % u == 0]
    if not unrolls:
        assert n_steps <= 32
        for n, ij in enumerate(order):
            lower_step(n, ij, order[max(n - 1, 0)], order[min(n + 1, n_steps - 1)])
        value_accumulate(p_ring[(n_steps - 1) % 3], *order[-1])
    elif n_steps > 0:
        u2 = unrolls[0]

        def advance(i, j):
            wrap = (j + 1) == i
            return jnp.where(wrap, i + 1, i), jnp.where(wrap, 0, j + 1)

        def softmax_lower_values(s, i):
            p_rows, al_rows = [], []
            for r0 in range(0, t, ATT_ROW_BLOCK):
                rows = pl.ds(r0, ATT_ROW_BLOCK)
                sb = [s[r0:r0 + ATT_ROW_BLOCK, c * lanes:(c + 1) * lanes] for c in range(n_col)]
                m_old = m_st[i, rows, :]
                m_new = jnp.maximum(
                    m_old, jnp.max(functools.reduce(jnp.maximum, sb), axis=-1, keepdims=True))
                p = [jnp.exp2(sc - m_new) for sc in sb]
                alpha = jnp.exp2(m_old - m_new)
                l_st[i, rows, :] = alpha * l_st[i, rows, :] + jnp.sum(
                    functools.reduce(jnp.add, p), axis=-1, keepdims=True)
                m_st[i, rows, :] = m_new
                p_rows.append(jnp.concatenate([pc.astype(BF16) for pc in p], axis=1))
                al_rows.append(alpha)
            return jnp.concatenate(p_rows, axis=0), jnp.concatenate(al_rows, axis=0)

        def lower_steps(it, carry):
            i, j, ip, jp = carry
            s_cur, p_prev, al_prev = s_a[...], p_c[...], al_c[...]
            for _ in range(u2):
                i_n, j_n = advance(i, j)
                s_nxt = lax.dot_general(q_ref[tile_rows(jnp.minimum(i_n, nq - 1)), :], k_ref[tile_rows(j_n), :],
                                        (((1,), (1,)), ((), ())), preferred_element_type=F32)
                p_cur, al_cur = softmax_lower_values(s_cur, i)
                acc_st[ip] = al_prev * acc_st[ip] + jnp.dot(
                    p_prev, v_ref[tile_rows(jp), :], preferred_element_type=F32)
                s_cur, p_prev, al_prev = s_nxt, p_cur, al_cur
                i, j, ip, jp = i_n, j_n, i, j
            s_a[...] = s_cur
            p_c[...] = p_prev
            al_c[...] = al_prev
            return i, j, ip, jp

        one = jnp.int32(1)
        zero = jnp.int32(0)
        _, _, ip, jp = lax.fori_loop(0, n_steps // u2, lower_steps, (one, zero, one, zero))
        value_accumulate(p_ring[2], ip, jp)

    def finish(i, carry):
        o_ref[tile_rows(i), :] = (acc_st[i] / l_st[i]).astype(o_ref.dtype)
        return carry

    lax.fori_loop(0, nq, finish, 0)


def _attention(qp, kp, v):
    B, S, _ = qp.shape
    t = min(S, ATT_TILE)
    nq = S // t
    lanes = V_HEAD_DIM
    head_block = lambda w: pl.BlockSpec((None, S, w), lambda b, h: (b, 0, h))
    return pl.pallas_call(
        _attn_kernel,
        grid=(B, ATT_HEADS),
        in_specs=[head_block(HEAD_PAD), head_block(HEAD_PAD), head_block(V_HEAD_DIM)],
        out_specs=head_block(V_HEAD_DIM),
        out_shape=jax.ShapeDtypeStruct((B, S, ATT_HEADS * V_HEAD_DIM), BF16),
        scratch_shapes=[pltpu.VMEM((t, t), F32)] * 3
                       + [pltpu.VMEM((t, t), BF16)] * 3
                       + [pltpu.VMEM((t, lanes), F32)] * 3
                       + [
                        pltpu.VMEM((nq, t, lanes), F32),
                        pltpu.VMEM((nq, t, lanes), F32),
                        pltpu.VMEM((nq, t, V_HEAD_DIM), F32)],
        compiler_params=pltpu.CompilerParams(
            dimension_semantics=("arbitrary", "arbitrary"), vmem_limit_bytes=VMEM_LIMIT),
        name="mla_attention",
    )(qp, kp, v)


def _layer_out_kernel(x_ref, y_ref, g_ref, gate_ref, wout_ref, gpost_ref, o_ref):
    aw = y_ref.shape[1]
    ya = (y_ref[...].astype(F32) * g_ref[:, 0:aw].astype(F32)).astype(BF16)
    y = (jnp.dot(ya, wout_ref[0:aw, :], preferred_element_type=F32)
         + jnp.dot(g_ref[:, aw:], wout_ref[aw:, :], preferred_element_type=F32))
    o_ref[...] = x_ref[...] + gate_ref[...] * _rms(y, gpost_ref[...])


def _layer_out(l, x, y_att, gated, mod, w_out_b, g_post, tm):
    B, S, D = x.shape
    seq = lambda w: pl.BlockSpec((None, tm, w), lambda b, i: (b, i, 0))
    return pl.pallas_call(
        _layer_out_kernel,
        grid=(B, S // tm),
        in_specs=[seq(D), seq(y_att.shape[-1]), seq(gated.shape[-1]), _mod_spec(l, 2, D),
                  _layer_spec(l, w_out_b), _layer_spec(l, g_post)],
        out_specs=seq(D),
        out_shape=jax.ShapeDtypeStruct((B, S, D), F32),
        compiler_params=pltpu.CompilerParams(
            dimension_semantics=("arbitrary", "arbitrary"), vmem_limit_bytes=VMEM_LIMIT),
        name="layer_out",
    )(x, y_att, gated, mod, w_out_b, g_post)


def _prep_w_in(w):
    pad = jnp.zeros(w.shape[:-1] + (64,), w.dtype)
    return jnp.concatenate([w[..., :384], w[..., 448:], w[..., 384:448], pad], axis=-1).astype(BF16)


def _prep_w_uq(w):
    lead = w.shape[:-1]
    w = w.reshape(lead + (ATT_HEADS, QK_NOPE_DIM + QK_ROPE_DIM))
    pad = jnp.zeros(lead + (ATT_HEADS, HEAD_PAD - QK_NOPE_DIM - QK_ROPE_DIM), w.dtype)
    return jnp.concatenate([w, pad], axis=-1).reshape(lead + (ATT_HEADS * HEAD_PAD,)).astype(BF16)


def _prep_w_ukv(w):
    lead = w.shape[:-1]
    w = w.reshape(lead + (ATT_HEADS, QK_NOPE_DIM + V_HEAD_DIM))
    return jnp.concatenate([w[..., :QK_NOPE_DIM].reshape(lead + (-1,)),
                            w[..., QK_NOPE_DIM:].reshape(lead + (-1,))], axis=-1).astype(BF16)


def kernel(x, c, positions, w_ada, b_ada, g_pre, g_post, w_in, q_norm_g, w_uq, kv_norm_g, w_ukv, conv_w,
           conv_b, conv_ln_g, conv_ln_b, w_pw2, sgu_ln_g, sgu_ln_b, w_s, b_s, w_out):
    B, S, D = x.shape
    L = w_ada.shape[0]
    tm = min(S, 512)

    c_pad = jnp.concatenate([c, jnp.zeros((8 - B % 8, D), c.dtype)], axis=0) if B % 8 else c
    mod = _modulation(c_pad, w_ada, b_ada)
    mod = mod.reshape(L, mod.shape[1], 3, 1, D)
    tcos, tsin = _rope_tables(positions)

    vec = lambda a: a.reshape(L, 1, -1)
    conv_w_p = jnp.concatenate([conv_w, jnp.zeros((L, 1, conv_w.shape[-1]), F32)], axis=1)
    bs_wide = jnp.repeat(jnp.swapaxes(b_s, 1, 2), w_pw2.shape[-1] // SGU_GROUPS, axis=2)
    params = (vec(g_pre), _prep_w_in(w_in), vec(q_norm_g), _prep_w_uq(w_uq), vec(kv_norm_g),
              _prep_w_ukv(w_ukv), conv_w_p, vec(conv_b), vec(conv_ln_g), vec(conv_ln_b),
              w_pw2.astype(BF16), vec(sgu_ln_g), vec(sgu_ln_b), w_s, bs_wide)
    w_out_b = w_out.astype(BF16)
    g_post = vec(g_post)

    for l in range(L):
        qp, kp, v, gated = _layer_in(l, x, mod, tcos, tsin, params, tm)
        y_att = _attention(qp, kp, v)
        x = _layer_out(l, x, y_att, gated, mod, w_out_b, g_post, tm)
    return x
```

```python
import functools

import numpy as np
import jax
import jax.numpy as jnp
from jax import lax
from jax.experimental import pallas as pl
from jax.experimental.pallas import tpu as pltpu

F32 = jnp.float32
BF16 = jnp.bfloat16

NORM_EPS = 1e-6
ATT_HEADS = 4
QK_NOPE_DIM = 128
QK_ROPE_DIM = 64
V_HEAD_DIM = 128
HEAD_PAD = 256
ROPE_THETA = 10000.0
CONV_KERNEL = 31
CONV_HALO = 32
CONV_ROW_BLOCK = 64
SGU_GROUPS = 4
SGU_CHUNK = 128
SQRT_HALF = float(np.sqrt(0.5).astype(np.float32))
ATT_SCALE = float((QK_NOPE_DIM + QK_ROPE_DIM) ** -0.5)
LOG2E = float(np.log2(np.e))
Q_SCALE = ATT_SCALE * LOG2E
VMEM_LIMIT = 56 * 1024 * 1024
LAYER_IN_TILE = 512
LAYER_OUT_TILE = 1024


def _rms(x, g):
    return x * lax.rsqrt(jnp.mean(x * x, axis=-1, keepdims=True) + NORM_EPS) * g


def _layer_norm(x, g, b):
    mu = jnp.mean(x, axis=-1, keepdims=True)
    xc = x - mu
    var = jnp.mean(xc * xc, axis=-1, keepdims=True)
    return xc * lax.rsqrt(var + NORM_EPS) * g + b


def _sigmoid(x):
    return 1.0 / (1.0 + jnp.exp(-x))


def _silu(x):
    return x * _sigmoid(x)


def _gelu(x):
    return 0.5 * x * (1.0 + lax.erf(x * SQRT_HALF))


def _rope(t, tc, ts):
    lane = lax.broadcasted_iota(jnp.int32, t.shape, 1)
    first = lane < (QK_ROPE_DIM // 2)
    sw = jnp.where(first, pltpu.roll(t, 128 - QK_ROPE_DIM // 2, 1), pltpu.roll(t, QK_ROPE_DIM // 2, 1))
    return t * tc + jnp.where(first, -sw, sw) * ts


def _mod_kernel(c_ref, w_ref, b_ref, o_ref):
    c = c_ref[...]
    o_ref[...] = jnp.dot(_silu(c), w_ref[...], preferred_element_type=F32) + b_ref[...]


def _modulation(c_pad, w_ada, b_ada):
    L, D, D3 = w_ada.shape
    tn = 1024
    return pl.pallas_call(
        _mod_kernel,
        grid=(L, D3 // tn),
        in_specs=[
            pl.BlockSpec(c_pad.shape, lambda l, j: (0, 0)),
            pl.BlockSpec((None, D, tn), lambda l, j: (l, 0, j)),
            pl.BlockSpec((None, 1, tn), lambda l, j: (l, 0, j)),
        ],
        out_specs=pl.BlockSpec((None, c_pad.shape[0], tn), lambda l, j: (l, 0, j)),
        out_shape=jax.ShapeDtypeStruct((L, c_pad.shape[0], D3), F32),
        compiler_params=pltpu.CompilerParams(
            dimension_semantics=("arbitrary", "arbitrary"), vmem_limit_bytes=VMEM_LIMIT),
        name="adaln_mod",
    )(c_pad, w_ada, b_ada.reshape(L, 1, D3))


def _rope_table_kernel(pos_ref, invf_ref, cos_ref, sin_ref):
    half = QK_ROPE_DIM // 2
    per_row = 128 // half
    tr = pos_ref.shape[0]
    ang = pos_ref[...].astype(F32) * invf_ref[...]
    lane = lax.broadcasted_iota(jnp.int32, ang.shape, 1)
    for tab, out_ref in ((jnp.cos(ang), cos_ref), (jnp.sin(ang), sin_ref)):
        for k in range(per_row):
            lo = tab if k == 0 else pltpu.roll(tab, 128 - half * k, 1)
            hi = pltpu.roll(lo, half, 1)
            wide = jnp.where(lane < half, lo, jnp.where(lane < 2 * half, hi, 0.0))
            out_ref[pl.ds(k, tr, stride=per_row), :] = wide


def _rope_tables(positions):
    B, S = positions.shape
    half = QK_ROPE_DIM // 2
    per_row = 128 // half
    n_rows = B * S // per_row
    inv_freq = ROPE_THETA ** (-jnp.arange(0, QK_ROPE_DIM, 2, dtype=F32) / QK_ROPE_DIM)
    pos_wide = jnp.broadcast_to(
        positions.reshape(n_rows, per_row, 1), (n_rows, per_row, half)).reshape(n_rows, 128)
    invf_wide = jnp.tile(inv_freq, per_row).reshape(1, 128)
    tr = min(n_rows, 512)
    cos_w, sin_w = pl.pallas_call(
        _rope_table_kernel,
        grid=(n_rows // tr,),
        in_specs=[pl.BlockSpec((tr, 128), lambda i: (i, 0)), pl.BlockSpec((1, 128), lambda i: (0, 0))],
        out_specs=[pl.BlockSpec((tr * per_row, 128), lambda i: (i, 0))] * 2,
        out_shape=[jax.ShapeDtypeStruct((B * S, 128), F32)] * 2,
        compiler_params=pltpu.CompilerParams(dimension_semantics=("arbitrary",)),
        name="rope_table",
    )(pos_wide, invf_wide)
    return cos_w.reshape(B, S, 128), sin_w.reshape(B, S, 128)


_C_QLAT, _C_KVLAT, _C_GATT, _C_CA, _C_CB, _C_GC, _C_SU, _C_SV, _C_GS, _C_KR, _C_END = (
    0, 256, 384, 896, 1152, 1408, 1664, 1920, 2176, 2432, 2560)


def _layer_in_kernel(x_ref, shift_ref, scale_ref, gpre_ref, win_ref, qg_ref, wuq_ref, kvg_ref, wukv_ref,
                     tc_ref, ts_ref, cw_ref, cb_ref, clg_ref, clb_ref, wpw_ref,
                     slg_ref, slb_ref, ws_ref, bs_ref,
                     q_out, k_out, v_out, g_out, hbuf, shifted):
    tm = x_ref.shape[0]

    @pl.when(pl.program_id(1) == 0)
    def _():
        hbuf[0:CONV_HALO, :] = jnp.zeros((CONV_HALO, hbuf.shape[1]), F32)

    x = x_ref[...]
    h = _rms(x, gpre_ref[...]) * (1.0 + scale_ref[...]) + shift_ref[...]
    hb = h.astype(BF16)

    def proj(a, b):
        return jnp.dot(hb, win_ref[:, a:b], preferred_element_type=F32)

    tc = tc_ref[...]
    ts = ts_ref[...]

    glu = proj(_C_CA, _C_CB) * _sigmoid(proj(_C_CB, _C_GC))
    hbuf[CONV_HALO:CONV_HALO + tm, :] = glu
    base = CONV_HALO - (CONV_KERNEL - 1)
    z_q = proj(_C_QLAT, _C_KVLAT)
    z_kv = proj(_C_KVLAT, _C_GATT)
    z_kr = proj(_C_KR, _C_END)
    for b in range(1, 8):
        shifted[b - 1, :, :] = hbuf[pl.ds(b, shifted.shape[1]), :]

    def conv_taps(r0):
        acc = jnp.broadcast_to(cb_ref[...], (CONV_ROW_BLOCK, glu.shape[1]))
        for k in range(CONV_KERNEL):
            a, b = divmod(base + k, 8)
            rows = pl.ds(r0 + 8 * a, CONV_ROW_BLOCK)
            src = hbuf[rows, :] if b == 0 else shifted[b - 1, rows, :]
            acc = acc + cw_ref[k:k + 1, :] * src
        return acc

    n_blk = tm // CONV_ROW_BLOCK
    taps = lambda part: [conv_taps(r * CONV_ROW_BLOCK)
                         for r in range(part * n_blk // 4, (part + 1) * n_blk // 4)]
    qn = _rms(z_q, qg_ref[...]).astype(BF16)
    kvn = _rms(z_kv, kvg_ref[...]).astype(BF16)
    q = jnp.dot(qn, wuq_ref[...], preferred_element_type=F32)
    kv = jnp.dot(kvn, wukv_ref[...], preferred_element_type=F32)
    blocks = taps(0) + taps(1)
    kr = _rope(z_kr, tc, ts).astype(BF16)
    for hd in range(ATT_HEADS):
        o = hd * HEAD_PAD
        q_out[:, o:o + QK_NOPE_DIM] = (q[:, o:o + QK_NOPE_DIM] * Q_SCALE).astype(BF16)
        q_out[:, o + QK_NOPE_DIM:o + HEAD_PAD] = (
            _rope(q[:, o + QK_NOPE_DIM:o + HEAD_PAD], tc, ts) * Q_SCALE).astype(BF16)
        k_out[:, o:o + QK_NOPE_DIM] = kv[:, hd * QK_NOPE_DIM:(hd + 1) * QK_NOPE_DIM].astype(BF16)
        k_out[:, o + QK_NOPE_DIM:o + HEAD_PAD] = kr
    v_out[...] = kv[:, ATT_HEADS * QK_NOPE_DIM:].astype(BF16)

    g_out[:, 0:512] = _silu(proj(_C_GATT, _C_CA)).astype(BF16)
    blocks += taps(2)

    vn = _layer_norm(_gelu(proj(_C_SV, _C_GS)), slg_ref[...], slb_ref[...]).astype(BF16)
    s_gc = _silu(proj(_C_GC, _C_SU))
    blocks += taps(3)
    hbuf[0:CONV_HALO, :] = hbuf[tm:tm + CONV_HALO, :]
    z_su = proj(_C_SU, _C_SV)
    z_gs = proj(_C_GS, _C_KR)

    hc = _silu(_layer_norm(jnp.concatenate(blocks, axis=0), clg_ref[...], clb_ref[...])).astype(BF16)
    yc = jnp.dot(hc, wpw_ref[...], preferred_element_type=F32) * s_gc
    g_out[:, 512:768] = yc.astype(BF16)

    r = lax.broadcasted_iota(jnp.int32, (SGU_CHUNK, SGU_CHUNK), 0)
    c = lax.broadcasted_iota(jnp.int32, (SGU_CHUNK, SGU_CHUNK), 1)
    w_stack = jnp.concatenate(
        [jnp.where(c <= r, ws_ref[g], 0.0) for g in range(SGU_GROUPS)], axis=0).astype(BF16)
    gw = vn.shape[1] // SGU_GROUPS
    lane_grp = lax.broadcasted_iota(jnp.int32, (SGU_CHUNK, vn.shape[1]), 1) // gw
    svs = []
    for ck in range(tm // SGU_CHUNK):
        full = jnp.dot(w_stack, vn[ck * SGU_CHUNK:(ck + 1) * SGU_CHUNK, :], preferred_element_type=F32)
        sv = full[0:SGU_CHUNK]
        for g in range(1, SGU_GROUPS):
            sv = jnp.where(lane_grp == g, full[g * SGU_CHUNK:(g + 1) * SGU_CHUNK], sv)
        svs.append(sv + bs_ref[...])
    sv = jnp.concatenate(svs, axis=0)
    ys = _gelu(z_su) * sv * _silu(z_gs)
    g_out[:, 768:1024] = ys.astype(BF16)


def _mod_spec(l, part, D):
    return pl.BlockSpec((None, None, None, 1, D), lambda b, i: (l, b, part, 0, 0))


def _layer_spec(l, a):
    return pl.BlockSpec((None,) + a.shape[1:], lambda b, i: (l,) + (0,) * (a.ndim - 1))


def _layer_in(l, x, mod, tcos, tsin, params, tm):
    B, S, D = x.shape
    cw = params[7].shape[-1]
    seq = lambda w: pl.BlockSpec((None, tm, w), lambda b, i: (b, i, 0))
    args = [x, mod, mod] + list(params[:6]) + [tcos, tsin] + list(params[6:])
    in_specs = ([seq(D), _mod_spec(l, 0, D), _mod_spec(l, 1, D)] + [_layer_spec(l, a) for a in params[:6]]
                + [seq(128), seq(128)] + [_layer_spec(l, a) for a in params[6:]])
    hp = ATT_HEADS * HEAD_PAD
    return pl.pallas_call(
        _layer_in_kernel,
        grid=(B, S // tm),
        in_specs=in_specs,
        out_specs=[seq(hp), seq(hp), seq(ATT_HEADS * V_HEAD_DIM), seq(D)],
        out_shape=[jax.ShapeDtypeStruct((B, S, hp), BF16), jax.ShapeDtypeStruct((B, S, hp), BF16),
                   jax.ShapeDtypeStruct((B, S, ATT_HEADS * V_HEAD_DIM), BF16),
                   jax.ShapeDtypeStruct((B, S, D), BF16)],
        scratch_shapes=[pltpu.VMEM((CONV_HALO + tm, cw), F32),
                        pltpu.VMEM((7, CONV_HALO + tm - 8, cw), F32)],
        compiler_params=pltpu.CompilerParams(
            dimension_semantics=("arbitrary", "arbitrary"), vmem_limit_bytes=VMEM_LIMIT),
        name="layer_in",
    )(*args)


ATT_TILE = 512
ATT_ROW_BLOCK = 16


def _attn_kernel(q_ref, k_ref, v_ref, o_ref, s_a, s_b, s_c, p_a, p_b, p_c, al_a, al_b, al_c, m_st, l_st, acc_st):
    t = s_a.shape[0]
    nq = q_ref.shape[0] // t
    lanes = m_st.shape[2]
    n_col = t // lanes
    s_ring = (s_a, s_b, s_c)
    p_ring = ((p_a, al_a), (p_b, al_b), (p_c, al_c))

    def tile_rows(idx):
        if isinstance(idx, int):
            return pl.ds(idx * t, t)
        return pl.ds(pl.multiple_of(idx * t, t), t)

    def score_tile(s_ref, i, j):
        s_ref[...] = lax.dot_general(q_ref[tile_rows(i), :], k_ref[tile_rows(j), :],
                                     (((1,), (1,)), ((), ())), preferred_element_type=F32)

    def softmax_diag(s_ref, p_buf, i):
        p_ref, _ = p_buf
        for r0 in range(0, t, ATT_ROW_BLOCK):
            rows = pl.ds(r0, ATT_ROW_BLOCK)
            live = [c for c in range(n_col) if c * lanes <= r0 + ATT_ROW_BLOCK - 1]
            s = {}
            for c in live:
                sc = s_ref[rows, c * lanes:(c + 1) * lanes]
                if (c + 1) * lanes - 1 > r0:
                    row = r0 + lax.broadcasted_iota(jnp.int32, (ATT_ROW_BLOCK, lanes), 0)
                    col = c * lanes + lax.broadcasted_iota(jnp.int32, (ATT_ROW_BLOCK, lanes), 1)
                    sc = jnp.where(col <= row, sc, -jnp.inf)
                s[c] = sc
            m_new = jnp.max(functools.reduce(jnp.maximum, s.values()), axis=-1, keepdims=True)
            p = {c: jnp.exp2(s[c] - m_new) for c in live}
            l_new = jnp.sum(functools.reduce(jnp.add, p.values()), axis=-1, keepdims=True)
            m_st[i, rows, :] = jnp.broadcast_to(m_new, (ATT_ROW_BLOCK, lanes))
            l_st[i, rows, :] = jnp.broadcast_to(l_new, (ATT_ROW_BLOCK, lanes))
            for c in range(n_col):
                p_ref[rows, c * lanes:(c + 1) * lanes] = (
                    p[c].astype(BF16) if c in p else jnp.zeros((ATT_ROW_BLOCK, lanes), BF16))

    def softmax_lower(s_ref, p_buf, i):
        p_ref, al_ref = p_buf
        for r0 in range(0, t, ATT_ROW_BLOCK):
            rows = pl.ds(r0, ATT_ROW_BLOCK)
            s = [s_ref[rows, c * lanes:(c + 1) * lanes] for c in range(n_col)]
            m_old = m_st[i, rows, :]
            m_new = jnp.maximum(m_old, jnp.max(functools.reduce(jnp.maximum, s), axis=-1, keepdims=True))
            p = [jnp.exp2(sc - m_new) for sc in s]
            alpha = jnp.exp2(m_old - m_new)
            l_st[i, rows, :] = alpha * l_st[i, rows, :] + jnp.sum(
                functools.reduce(jnp.add, p), axis=-1, keepdims=True)
            m_st[i, rows, :] = m_new
            al_ref[rows, :] = alpha
            for c in range(n_col):
                p_ref[rows, c * lanes:(c + 1) * lanes] = p[c].astype(BF16)

    def value_first(p_buf, i, j):
        acc_st[i] = jnp.dot(p_buf[0][...], v_ref[tile_rows(j), :], preferred_element_type=F32)

    def value_accumulate(p_buf, i, j):
        acc_st[i] = p_buf[1][...] * acc_st[i] + jnp.dot(
            p_buf[0][...], v_ref[tile_rows(j), :], preferred_element_type=F32)

    if nq == 1:
        score_tile(s_ring[0], 0, 0)
        softmax_diag(s_ring[0], p_ring[0], 0)
        value_first(p_ring[0], 0, 0)
    else:
        u1 = 4 if nq % 4 == 0 else 2
        assert nq % u1 == 0
        p_b[...] = jnp.zeros(p_b.shape, BF16)
        score_tile(s_ring[0], 0, 0)

        def diag_steps(it, carry):
            for d in range(u1):
                i = it * u1 + d
                cur, other = d % 2, (d + 1) % 2
                nxt = jnp.minimum(i + 1, nq - 1)
                prv = jnp.maximum(i - 1, 0)
                score_tile(s_ring[other], nxt, nxt)
                softmax_diag(s_ring[cur], p_ring[cur], i)
                value_first(p_ring[other], prv, prv)
            return carry

        lax.fori_loop(0, nq // u1, diag_steps, 0)
        value_first(p_ring[(nq - 1) % 2], nq - 1, nq - 1)

    order = [(i, j) for i in range(1, nq) for j in range(i)]
    n_steps = len(order)

    def lower_step(n, cur_ij, prev_ij, next_ij):
        score_tile(s_ring[(n + 1) % 3], *next_ij)
        softmax_lower(s_ring[n % 3], p_ring[n % 3], cur_ij[0])
        value_accumulate(p_ring[(n - 1) % 3], *prev_ij)

    if n_steps > 0:
        p_c[...] = jnp.zeros(p_c.shape, BF16)
        al_c[...] = jnp.ones(al_c.shape, F32)
        score_tile(s_ring[0], 1, 0)

    unrolls = [u for u in (12, 6, 3) if n_steps % u == 0]
    if not unrolls:
        assert n_steps <= 32
        for n, ij in enumerate(order):
            lower_step(n, ij, order[max(n - 1, 0)], order[min(n + 1, n_steps - 1)])
        value_accumulate(p_ring[(n_steps - 1) % 3], *order[-1])
    elif n_steps > 0:
        u2 = unrolls[0]

        def advance(i, j):
            wrap = (j + 1) == i
            return jnp.where(wrap, i + 1, i), jnp.where(wrap, 0, j + 1)

        def softmax_lower_values(s, i):
            p_rows, al_rows = [], []
            for r0 in range(0, t, ATT_ROW_BLOCK):
                rows = pl.ds(r0, ATT_ROW_BLOCK)
                sb = [s[r0:r0 + ATT_ROW_BLOCK, c * lanes:(c + 1) * lanes] for c in range(n_col)]
                m_old = m_st[i, rows, :]
                m_new = jnp.maximum(
                    m_old, jnp.max(functools.reduce(jnp.maximum, sb), axis=-1, keepdims=True))
                p = [jnp.exp2(sc - m_new) for sc in sb]
                alpha = jnp.exp2(m_old - m_new)
                l_st[i, rows, :] = alpha * l_st[i, rows, :] + jnp.sum(
                    functools.reduce(jnp.add, p), axis=-1, keepdims=True)
                m_st[i, rows, :] = m_new
                p_rows.append(jnp.concatenate([pc.astype(BF16) for pc in p], axis=1))
                al_rows.append(alpha)
            return jnp.concatenate(p_rows, axis=0), jnp.concatenate(al_rows, axis=0)

        def lower_steps(it, carry):
            i, j, ip, jp = carry
            s_cur, p_prev, al_prev = s_a[...], p_c[...], al_c[...]
            for _ in range(u2):
                i_n, j_n = advance(i, j)
                s_nxt = lax.dot_general(q_ref[tile_rows(jnp.minimum(i_n, nq - 1)), :], k_ref[tile_rows(j_n), :],
                                        (((1,), (1,)), ((), ())), preferred_element_type=F32)
                p_cur, al_cur = softmax_lower_values(s_cur, i)
                acc_st[ip] = al_prev * acc_st[ip] + jnp.dot(
                    p_prev, v_ref[tile_rows(jp), :], preferred_element_type=F32)
                s_cur, p_prev, al_prev = s_nxt, p_cur, al_cur
                i, j, ip, jp = i_n, j_n, i, j
            s_a[...] = s_cur
            p_c[...] = p_prev
            al_c[...] = al_prev
            return i, j, ip, jp

        one = jnp.int32(1)
        zero = jnp.int32(0)
        _, _, ip, jp = lax.fori_loop(0, n_steps // u2, lower_steps, (one, zero, one, zero))
        value_accumulate(p_ring[2], ip, jp)

    def finish(i, carry):
        o_ref[tile_rows(i), :] = (acc_st[i] / l_st[i]).astype(o_ref.dtype)
        return carry

    lax.fori_loop(0, nq, finish, 0)


def _attention(qp, kp, v):
    B, S, _ = qp.shape
    t = min(S, ATT_TILE)
    nq = S // t
    lanes = V_HEAD_DIM
    head_block = lambda w: pl.BlockSpec((None, S, w), lambda b, h: (b, 0, h))
    return pl.pallas_call(
        _attn_kernel,
        grid=(B, ATT_HEADS),
        in_specs=[head_block(HEAD_PAD), head_block(HEAD_PAD), head_block(V_HEAD_DIM)],
        out_specs=head_block(V_HEAD_DIM),
        out_shape=jax.ShapeDtypeStruct((B, S, ATT_HEADS * V_HEAD_DIM), BF16),
        scratch_shapes=[pltpu.VMEM((t, t), F32)] * 3
                       + [pltpu.VMEM((t, t), BF16)] * 3
                       + [pltpu.VMEM((t, lanes), F32)] * 3
                       + [
                        pltpu.VMEM((nq, t, lanes), F32),
                        pltpu.VMEM((nq, t, lanes), F32),
                        pltpu.VMEM((nq, t, V_HEAD_DIM), F32)],
        compiler_params=pltpu.CompilerParams(
            dimension_semantics=("arbitrary", "arbitrary"), vmem_limit_bytes=VMEM_LIMIT),
        name="mla_attention",
    )(qp, kp, v)


def _layer_out_kernel(x_ref, y_ref, g_ref, gate_ref, wout_ref, gpost_ref, o_ref):
    aw = y_ref.shape[1]
    ya = (y_ref[...].astype(F32) * g_ref[:, 0:aw].astype(F32)).astype(BF16)
    y = (jnp.dot(ya, wout_ref[0:aw, :], preferred_element_type=F32)
         + jnp.dot(g_ref[:, aw:], wout_ref[aw:, :], preferred_element_type=F32))
    o_ref[...] = x_ref[...] + gate_ref[...] * _rms(y, gpost_ref[...])


def _layer_out(l, x, y_att, gated, mod, w_out_b, g_post, tm):
    B, S, D = x.shape
    seq = lambda w: pl.BlockSpec((None, tm, w), lambda b, i: (b, i, 0))
    return pl.pallas_call(
        _layer_out_kernel,
        grid=(B, S // tm),
        in_specs=[seq(D), seq(y_att.shape[-1]), seq(gated.shape[-1]), _mod_spec(l, 2, D),
                  _layer_spec(l, w_out_b), _layer_spec(l, g_post)],
        out_specs=seq(D),
        out_shape=jax.ShapeDtypeStruct((B, S, D), F32),
        compiler_params=pltpu.CompilerParams(
            dimension_semantics=("arbitrary", "arbitrary"), vmem_limit_bytes=VMEM_LIMIT),
        name="layer_out",
    )(x, y_att, gated, mod, w_out_b, g_post)


def _prep_w_in(w):
    pad = jnp.zeros(w.shape[:-1] + (64,), w.dtype)
    return jnp.concatenate([w[..., :384], w[..., 448:], w[..., 384:448], pad], axis=-1).astype(BF16)


def _prep_w_uq(w):
    lead = w.shape[:-1]
    w = w.reshape(lead + (ATT_HEADS, QK_NOPE_DIM + QK_ROPE_DIM))
    pad = jnp.zeros(lead + (ATT_HEADS, HEAD_PAD - QK_NOPE_DIM - QK_ROPE_DIM), w.dtype)
    return jnp.concatenate([w, pad], axis=-1).reshape(lead + (ATT_HEADS * HEAD_PAD,)).astype(BF16)


def _prep_w_ukv(w):
    lead = w.shape[:-1]
    w = w.reshape(lead + (ATT_HEADS, QK_NOPE_DIM + V_HEAD_DIM))
    return jnp.concatenate([w[..., :QK_NOPE_DIM].reshape(lead + (-1,)),
                            w[..., QK_NOPE_DIM:].reshape(lead + (-1,))], axis=-1).astype(BF16)


def kernel(x, c, positions, w_ada, b_ada, g_pre, g_post, w_in, q_norm_g, w_uq, kv_norm_g, w_ukv, conv_w,
           conv_b, conv_ln_g, conv_ln_b, w_pw2, sgu_ln_g, sgu_ln_b, w_s, b_s, w_out):
    B, S, D = x.shape
    L = w_ada.shape[0]
    tm = min(S, LAYER_IN_TILE)

    c_pad = jnp.concatenate([c, jnp.zeros((8 - B % 8, D), c.dtype)], axis=0) if B % 8 else c
    mod = _modulation(c_pad, w_ada, b_ada)
    mod = mod.reshape(L, mod.shape[1], 3, 1, D)
    tcos, tsin = _rope_tables(positions)

    vec = lambda a: a.reshape(L, 1, -1)
    conv_w_p = jnp.concatenate([conv_w, jnp.zeros((L, 1, conv_w.shape[-1]), F32)], axis=1)
    bs_wide = jnp.repeat(jnp.swapaxes(b_s, 1, 2), w_pw2.shape[-1] // SGU_GROUPS, axis=2)
    params = (vec(g_pre), _prep_w_in(w_in), vec(q_norm_g), _prep_w_uq(w_uq), vec(kv_norm_g),
              _prep_w_ukv(w_ukv), conv_w_p, vec(conv_b), vec(conv_ln_g), vec(conv_ln_b),
              w_pw2.astype(BF16), vec(sgu_ln_g), vec(sgu_ln_b), w_s, bs_wide)
    w_out_b = w_out.astype(BF16)
    g_post = vec(g_post)

    for l in range(L):
        qp, kp, v, gated = _layer_in(l, x, mod, tcos, tsin, params, tm)
        y_att = _attention(qp, kp, v)
        x = _layer_out(l, x, y_att, gated, mod, w_out_b, g_post, min(S, LAYER_OUT_TILE))
    return x
```

```python
import functools

import numpy as np
import jax
import jax.numpy as jnp
from jax import lax
from jax.experimental import pallas as pl
from jax.experimental.pallas import tpu as pltpu

F32 = jnp.float32
BF16 = jnp.bfloat16

NORM_EPS = 1e-6
ATT_HEADS = 4
QK_NOPE_DIM = 128
QK_ROPE_DIM = 64
V_HEAD_DIM = 128
HEAD_PAD = 256
ROPE_THETA = 10000.0
CONV_KERNEL = 31
CONV_HALO = 32
CONV_ROW_BLOCK = 64
SGU_GROUPS = 4
SGU_CHUNK = 128
SQRT_HALF = float(np.sqrt(0.5).astype(np.float32))
ATT_SCALE = float((QK_NOPE_DIM + QK_ROPE_DIM) ** -0.5)
LOG2E = float(np.log2(np.e))
Q_SCALE = ATT_SCALE * LOG2E
VMEM_LIMIT = 56 * 1024 * 1024
LAYER_IN_TILE = 1024
LAYER_OUT_TILE = 1024


def _rms(x, g):
    return x * lax.rsqrt(jnp.mean(x * x, axis=-1, keepdims=True) + NORM_EPS) * g


def _layer_norm(x, g, b):
    mu = jnp.mean(x, axis=-1, keepdims=True)
    xc = x - mu
    var = jnp.mean(xc * xc, axis=-1, keepdims=True)
    return xc * lax.rsqrt(var + NORM_EPS) * g + b


def _sigmoid(x):
    return 1.0 / (1.0 + jnp.exp(-x))


def _silu(x):
    return x * _sigmoid(x)


def _gelu(x):
    return 0.5 * x * (1.0 + lax.erf(x * SQRT_HALF))


def _rope(t, tc, ts):
    lane = lax.broadcasted_iota(jnp.int32, t.shape, 1)
    first = lane < (QK_ROPE_DIM // 2)
    sw = jnp.where(first, pltpu.roll(t, 128 - QK_ROPE_DIM // 2, 1), pltpu.roll(t, QK_ROPE_DIM // 2, 1))
    return t * tc + jnp.where(first, -sw, sw) * ts


def _mod_kernel(c_ref, w_ref, b_ref, o_ref):
    c = c_ref[...]
    o_ref[...] = jnp.dot(_silu(c), w_ref[...], preferred_element_type=F32) + b_ref[...]


def _modulation(c_pad, w_ada, b_ada):
    L, D, D3 = w_ada.shape
    tn = 1024
    return pl.pallas_call(
        _mod_kernel,
        grid=(L, D3 // tn),
        in_specs=[
            pl.BlockSpec(c_pad.shape, lambda l, j: (0, 0)),
            pl.BlockSpec((None, D, tn), lambda l, j: (l, 0, j)),
            pl.BlockSpec((None, 1, tn), lambda l, j: (l, 0, j)),
        ],
        out_specs=pl.BlockSpec((None, c_pad.shape[0], tn), lambda l, j: (l, 0, j)),
        out_shape=jax.ShapeDtypeStruct((L, c_pad.shape[0], D3), F32),
        compiler_params=pltpu.CompilerParams(
            dimension_semantics=("arbitrary", "arbitrary"), vmem_limit_bytes=VMEM_LIMIT),
        name="adaln_mod",
    )(c_pad, w_ada, b_ada.reshape(L, 1, D3))


def _rope_table_kernel(pos_ref, invf_ref, cos_ref, sin_ref):
    half = QK_ROPE_DIM // 2
    per_row = 128 // half
    tr = pos_ref.shape[0]
    ang = pos_ref[...].astype(F32) * invf_ref[...]
    lane = lax.broadcasted_iota(jnp.int32, ang.shape, 1)
    for tab, out_ref in ((jnp.cos(ang), cos_ref), (jnp.sin(ang), sin_ref)):
        for k in range(per_row):
            lo = tab if k == 0 else pltpu.roll(tab, 128 - half * k, 1)
            hi = pltpu.roll(lo, half, 1)
            wide = jnp.where(lane < half, lo, jnp.where(lane < 2 * half, hi, 0.0))
            out_ref[pl.ds(k, tr, stride=per_row), :] = wide


def _rope_tables(positions):
    B, S = positions.shape
    half = QK_ROPE_DIM // 2
    per_row = 128 // half
    n_rows = B * S // per_row
    inv_freq = ROPE_THETA ** (-jnp.arange(0, QK_ROPE_DIM, 2, dtype=F32) / QK_ROPE_DIM)
    pos_wide = jnp.broadcast_to(
        positions.reshape(n_rows, per_row, 1), (n_rows, per_row, half)).reshape(n_rows, 128)
    invf_wide = jnp.tile(inv_freq, per_row).reshape(1, 128)
    tr = min(n_rows, 512)
    cos_w, sin_w = pl.pallas_call(
        _rope_table_kernel,
        grid=(n_rows // tr,),
        in_specs=[pl.BlockSpec((tr, 128), lambda i: (i, 0)), pl.BlockSpec((1, 128), lambda i: (0, 0))],
        out_specs=[pl.BlockSpec((tr * per_row, 128), lambda i: (i, 0))] * 2,
        out_shape=[jax.ShapeDtypeStruct((B * S, 128), F32)] * 2,
        compiler_params=pltpu.CompilerParams(dimension_semantics=("arbitrary",)),
        name="rope_table",
    )(pos_wide, invf_wide)
    return cos_w.reshape(B, S, 128), sin_w.reshape(B, S, 128)


_C_QLAT, _C_KVLAT, _C_GATT, _C_CA, _C_CB, _C_GC, _C_SU, _C_SV, _C_GS, _C_KR, _C_END = (
    0, 256, 384, 896, 1152, 1408, 1664, 1920, 2176, 2432, 2560)


def _layer_in_kernel(x_ref, shift_ref, scale_ref, gpre_ref, win_ref, qg_ref, wuq_ref, kvg_ref, wukv_ref,
                     tc_ref, ts_ref, cw_ref, cb_ref, clg_ref, clb_ref, wpw_ref,
                     slg_ref, slb_ref, ws_ref, bs_ref,
                     q_out, k_out, v_out, g_out, hbuf, shifted):
    tm = x_ref.shape[0]

    @pl.when(pl.program_id(1) == 0)
    def _():
        hbuf[0:CONV_HALO, :] = jnp.zeros((CONV_HALO, hbuf.shape[1]), F32)

    x = x_ref[...]
    h = _rms(x, gpre_ref[...]) * (1.0 + scale_ref[...]) + shift_ref[...]
    hb = h.astype(BF16)

    def proj(a, b):
        return jnp.dot(hb, win_ref[:, a:b], preferred_element_type=F32)

    tc = tc_ref[...]
    ts = ts_ref[...]

    glu = proj(_C_CA, _C_CB) * _sigmoid(proj(_C_CB, _C_GC))
    hbuf[CONV_HALO:CONV_HALO + tm, :] = glu
    base = CONV_HALO - (CONV_KERNEL - 1)
    z_q = proj(_C_QLAT, _C_KVLAT)
    z_kv = proj(_C_KVLAT, _C_GATT)
    z_kr = proj(_C_KR, _C_END)
    for b in range(1, 8):
        shifted[b - 1, :, :] = hbuf[pl.ds(b, shifted.shape[1]), :]

    def conv_taps(r0):
        acc = jnp.broadcast_to(cb_ref[...], (CONV_ROW_BLOCK, glu.shape[1]))
        for k in range(CONV_KERNEL):
            a, b = divmod(base + k, 8)
            rows = pl.ds(r0 + 8 * a, CONV_ROW_BLOCK)
            src = hbuf[rows, :] if b == 0 else shifted[b - 1, rows, :]
            acc = acc + cw_ref[k:k + 1, :] * src
        return acc

    n_blk = tm // CONV_ROW_BLOCK
    taps = lambda part: [conv_taps(r * CONV_ROW_BLOCK)
                         for r in range(part * n_blk // 4, (part + 1) * n_blk // 4)]
    qn = _rms(z_q, qg_ref[...]).astype(BF16)
    kvn = _rms(z_kv, kvg_ref[...]).astype(BF16)
    q = jnp.dot(qn, wuq_ref[...], preferred_element_type=F32)
    kv = jnp.dot(kvn, wukv_ref[...], preferred_element_type=F32)
    blocks = taps(0) + taps(1)
    kr = _rope(z_kr, tc, ts).astype(BF16)
    for hd in range(ATT_HEADS):
        o = hd * HEAD_PAD
        q_out[:, o:o + QK_NOPE_DIM] = (q[:, o:o + QK_NOPE_DIM] * Q_SCALE).astype(BF16)
        q_out[:, o + QK_NOPE_DIM:o + HEAD_PAD] = (
            _rope(q[:, o + QK_NOPE_DIM:o + HEAD_PAD], tc, ts) * Q_SCALE).astype(BF16)
        k_out[:, o:o + QK_NOPE_DIM] = kv[:, hd * QK_NOPE_DIM:(hd + 1) * QK_NOPE_DIM].astype(BF16)
        k_out[:, o + QK_NOPE_DIM:o + HEAD_PAD] = kr
    v_out[...] = kv[:, ATT_HEADS * QK_NOPE_DIM:].astype(BF16)

    g_out[:, 0:512] = _silu(proj(_C_GATT, _C_CA)).astype(BF16)
    blocks += taps(2)

    vn = _layer_norm(_gelu(proj(_C_SV, _C_GS)), slg_ref[...], slb_ref[...]).astype(BF16)
    s_gc = _silu(proj(_C_GC, _C_SU))
    blocks += taps(3)
    hbuf[0:CONV_HALO, :] = hbuf[tm:tm + CONV_HALO, :]
    z_su = proj(_C_SU, _C_SV)
    z_gs = proj(_C_GS, _C_KR)

    hc = _silu(_layer_norm(jnp.concatenate(blocks, axis=0), clg_ref[...], clb_ref[...])).astype(BF16)
    yc = jnp.dot(hc, wpw_ref[...], preferred_element_type=F32) * s_gc
    g_out[:, 512:768] = yc.astype(BF16)

    r = lax.broadcasted_iota(jnp.int32, (SGU_CHUNK, SGU_CHUNK), 0)
    c = lax.broadcasted_iota(jnp.int32, (SGU_CHUNK, SGU_CHUNK), 1)
    w_stack = jnp.concatenate(
        [jnp.where(c <= r, ws_ref[g], 0.0) for g in range(SGU_GROUPS)], axis=0).astype(BF16)
    gw = vn.shape[1] // SGU_GROUPS
    lane_grp = lax.broadcasted_iota(jnp.int32, (SGU_CHUNK, vn.shape[1]), 1) // gw
    svs = []
    for ck in range(tm // SGU_CHUNK):
        full = jnp.dot(w_stack, vn[ck * SGU_CHUNK:(ck + 1) * SGU_CHUNK, :], preferred_element_type=F32)
        sv = full[0:SGU_CHUNK]
        for g in range(1, SGU_GROUPS):
            sv = jnp.where(lane_grp == g, full[g * SGU_CHUNK:(g + 1) * SGU_CHUNK], sv)
        svs.append(sv + bs_ref[...])
    sv = jnp.concatenate(svs, axis=0)
    ys = _gelu(z_su) * sv * _silu(z_gs)
    g_out[:, 768:1024] = ys.astype(BF16)


def _mod_spec(l, part, D):
    return pl.BlockSpec((None, None, None, 1, D), lambda b, i: (l, b, part, 0, 0))


def _layer_spec(l, a):
    return pl.BlockSpec((None,) + a.shape[1:], lambda b, i: (l,) + (0,) * (a.ndim - 1))


def _layer_in(l, x, mod, tcos, tsin, params, tm):
    B, S, D = x.shape
    cw = params[7].shape[-1]
    seq = lambda w: pl.BlockSpec((None, tm, w), lambda b, i: (b, i, 0))
    args = [x, mod, mod] + list(params[:6]) + [tcos, tsin] + list(params[6:])
    in_specs = ([seq(D), _mod_spec(l, 0, D), _mod_spec(l, 1, D)] + [_layer_spec(l, a) for a in params[:6]]
                + [seq(128), seq(128)] + [_layer_spec(l, a) for a in params[6:]])
    hp = ATT_HEADS * HEAD_PAD
    return pl.pallas_call(
        _layer_in_kernel,
        grid=(B, S // tm),
        in_specs=in_specs,
        out_specs=[seq(hp), seq(hp), seq(ATT_HEADS * V_HEAD_DIM), seq(D)],
        out_shape=[jax.ShapeDtypeStruct((B, S, hp), BF16), jax.ShapeDtypeStruct((B, S, hp), BF16),
                   jax.ShapeDtypeStruct((B, S, ATT_HEADS * V_HEAD_DIM), BF16),
                   jax.ShapeDtypeStruct((B, S, D), BF16)],
        scratch_shapes=[pltpu.VMEM((CONV_HALO + tm, cw), F32),
                        pltpu.VMEM((7, CONV_HALO + tm - 8, cw), F32)],
        compiler_params=pltpu.CompilerParams(
            dimension_semantics=("arbitrary", "arbitrary"), vmem_limit_bytes=VMEM_LIMIT),
        name="layer_in",
    )(*args)


ATT_TILE = 512
ATT_ROW_BLOCK = 16


def _attn_kernel(q_ref, k_ref, v_ref, o_ref, s_a, s_b, s_c, p_a, p_b, p_c, al_a, al_b, al_c, m_st, l_st, acc_st):
    t = s_a.shape[0]
    nq = q_ref.shape[0] // t
    lanes = m_st.shape[2]
    n_col = t // lanes
    s_ring = (s_a, s_b, s_c)
    p_ring = ((p_a, al_a), (p_b, al_b), (p_c, al_c))

    def tile_rows(idx):
        if isinstance(idx, int):
            return pl.ds(idx * t, t)
        return pl.ds(pl.multiple_of(idx * t, t), t)

    def score_tile(s_ref, i, j):
        s_ref[...] = lax.dot_general(q_ref[tile_rows(i), :], k_ref[tile_rows(j), :],
                                     (((1,), (1,)), ((), ())), preferred_element_type=F32)

    def softmax_diag(s_ref, p_buf, i):
        p_ref, _ = p_buf
        for r0 in range(0, t, ATT_ROW_BLOCK):
            rows = pl.ds(r0, ATT_ROW_BLOCK)
            live = [c for c in range(n_col) if c * lanes <= r0 + ATT_ROW_BLOCK - 1]
            s = {}
            for c in live:
                sc = s_ref[rows, c * lanes:(c + 1) * lanes]
                if (c + 1) * lanes - 1 > r0:
                    row = r0 + lax.broadcasted_iota(jnp.int32, (ATT_ROW_BLOCK, lanes), 0)
                    col = c * lanes + lax.broadcasted_iota(jnp.int32, (ATT_ROW_BLOCK, lanes), 1)
                    sc = jnp.where(col <= row, sc, -jnp.inf)
                s[c] = sc
            m_new = jnp.max(functools.reduce(jnp.maximum, s.values()), axis=-1, keepdims=True)
            p = {c: jnp.exp2(s[c] - m_new) for c in live}
            l_new = jnp.sum(functools.reduce(jnp.add, p.values()), axis=-1, keepdims=True)
            m_st[i, rows, :] = jnp.broadcast_to(m_new, (ATT_ROW_BLOCK, lanes))
            l_st[i, rows, :] = jnp.broadcast_to(l_new, (ATT_ROW_BLOCK, lanes))
            for c in range(n_col):
                p_ref[rows, c * lanes:(c + 1) * lanes] = (
                    p[c].astype(BF16) if c in p else jnp.zeros((ATT_ROW_BLOCK, lanes), BF16))

    def softmax_lower(s_ref, p_buf, i):
        p_ref, al_ref = p_buf
        for r0 in range(0, t, ATT_ROW_BLOCK):
            rows = pl.ds(r0, ATT_ROW_BLOCK)
            s = [s_ref[rows, c * lanes:(c + 1) * lanes] for c in range(n_col)]
            m_old = m_st[i, rows, :]
            m_new = jnp.maximum(m_old, jnp.max(functools.reduce(jnp.maximum, s), axis=-1, keepdims=True))
            p = [jnp.exp2(sc - m_new) for sc in s]
            alpha = jnp.exp2(m_old - m_new)
            l_st[i, rows, :] = alpha * l_st[i, rows, :] + jnp.sum(
                functools.reduce(jnp.add, p), axis=-1, keepdims=True)
            m_st[i, rows, :] = m_new
            al_ref[rows, :] = alpha
            for c in range(n_col):
                p_ref[rows, c * lanes:(c + 1) * lanes] = p[c].astype(BF16)

    def value_first(p_buf, i, j):
        acc_st[i] = jnp.dot(p_buf[0][...], v_ref[tile_rows(j), :], preferred_element_type=F32)

    def value_accumulate(p_buf, i, j):
        acc_st[i] = p_buf[1][...] * acc_st[i] + jnp.dot(
            p_buf[0][...], v_ref[tile_rows(j), :], preferred_element_type=F32)

    if nq == 1:
        score_tile(s_ring[0], 0, 0)
        softmax_diag(s_ring[0], p_ring[0], 0)
        value_first(p_ring[0], 0, 0)
    else:
        u1 = 4 if nq % 4 == 0 else 2
        assert nq % u1 == 0
        p_b[...] = jnp.zeros(p_b.shape, BF16)
        score_tile(s_ring[0], 0, 0)

        def diag_steps(it, carry):
            for d in range(u1):
                i = it * u1 + d
                cur, other = d % 2, (d + 1) % 2
                nxt = jnp.minimum(i + 1, nq - 1)
                prv = jnp.maximum(i - 1, 0)
                score_tile(s_ring[other], nxt, nxt)
                softmax_diag(s_ring[cur], p_ring[cur], i)
                value_first(p_ring[other], prv, prv)
            return carry

        lax.fori_loop(0, nq // u1, diag_steps, 0)
        value_first(p_ring[(nq - 1) % 2], nq - 1, nq - 1)

    order = [(i, j) for i in range(1, nq) for j in range(i)]
    n_steps = len(order)

    def lower_step(n, cur_ij, prev_ij, next_ij):
        score_tile(s_ring[(n + 1) % 3], *next_ij)
        softmax_lower(s_ring[n % 3], p_ring[n % 3], cur_ij[0])
        value_accumulate(p_ring[(n - 1) % 3], *prev_ij)

    if n_steps > 0:
        p_c[...] = jnp.zeros(p_c.shape, BF16)
        al_c[...] = jnp.ones(al_c.shape, F32)
        score_tile(s_ring[0], 1, 0)

    unrolls = [u for u in (12, 6, 3) if n_steps % u == 0]
    if not unrolls:
        assert n_steps <= 32
        for n, ij in enumerate(order):
            lower_step(n, ij, order[max(n - 1, 0)], order[min(n + 1, n_steps - 1)])
        value_accumulate(p_ring[(n_steps - 1) % 3], *order[-1])
    elif n_steps > 0:
        u2 = unrolls[0]

        def advance(i, j):
            wrap = (j + 1) == i
            return jnp.where(wrap, i + 1, i), jnp.where(wrap, 0, j + 1)

        def softmax_lower_values(s, i):
            p_rows, al_rows = [], []
            for r0 in range(0, t, ATT_ROW_BLOCK):
                rows = pl.ds(r0, ATT_ROW_BLOCK)
                sb = [s[r0:r0 + ATT_ROW_BLOCK, c * lanes:(c + 1) * lanes] for c in range(n_col)]
                m_old = m_st[i, rows, :]
                m_new = jnp.maximum(
                    m_old, jnp.max(functools.reduce(jnp.maximum, sb), axis=-1, keepdims=True))
                p = [jnp.exp2(sc - m_new) for sc in sb]
                alpha = jnp.exp2(m_old - m_new)
                l_st[i, rows, :] = alpha * l_st[i, rows, :] + jnp.sum(
                    functools.reduce(jnp.add, p), axis=-1, keepdims=True)
                m_st[i, rows, :] = m_new
                p_rows.append(jnp.concatenate([pc.astype(BF16) for pc in p], axis=1))
                al_rows.append(alpha)
            return jnp.concatenate(p_rows, axis=0), jnp.concatenate(al_rows, axis=0)

        def lower_steps(it, carry):
            i, j, ip, jp = carry
            s_cur, p_prev, al_prev = s_a[...], p_c[...], al_c[...]
            for _ in range(u2):
                i_n, j_n = advance(i, j)
                s_nxt = lax.dot_general(q_ref[tile_rows(jnp.minimum(i_n, nq - 1)), :], k_ref[tile_rows(j_n), :],
                                        (((1,), (1,)), ((), ())), preferred_element_type=F32)
                p_cur, al_cur = softmax_lower_values(s_cur, i)
                acc_st[ip] = al_prev * acc_st[ip] + jnp.dot(
                    p_prev, v_ref[tile_rows(jp), :], preferred_element_type=F32)
                s_cur, p_prev, al_prev = s_nxt, p_cur, al_cur
                i, j, ip, jp = i_n, j_n, i, j
            s_a[...] = s_cur
            p_c[...] = p_prev
            al_c[...] = al_prev
            return i, j, ip, jp

        one = jnp.int32(1)
        zero = jnp.int32(0)
        _, _, ip, jp = lax.fori_loop(0, n_steps // u2, lower_steps, (one, zero, one, zero))
        value_accumulate(p_ring[2], ip, jp)

    def finish(i, carry):
        o_ref[tile_rows(i), :] = (acc_st[i] / l_st[i]).astype(o_ref.dtype)
        return carry

    lax.fori_loop(0, nq, finish, 0)


def _attention(qp, kp, v):
    B, S, _ = qp.shape
    t = min(S, ATT_TILE)
    nq = S // t
    lanes = V_HEAD_DIM
    head_block = lambda w: pl.BlockSpec((None, S, w), lambda b, h: (b, 0, h))
    return pl.pallas_call(
        _attn_kernel,
        grid=(B, ATT_HEADS),
        in_specs=[head_block(HEAD_PAD), head_block(HEAD_PAD), head_block(V_HEAD_DIM)],
        out_specs=head_block(V_HEAD_DIM),
        out_shape=jax.ShapeDtypeStruct((B, S, ATT_HEADS * V_HEAD_DIM), BF16),
        scratch_shapes=[pltpu.VMEM((t, t), F32)] * 3
                       + [pltpu.VMEM((t, t), BF16)] * 3
                       + [pltpu.VMEM((t, lanes), F32)] * 3
                       + [
                        pltpu.VMEM((nq, t, lanes), F32),
                        pltpu.VMEM((nq, t, lanes), F32),
                        pltpu.VMEM((nq, t, V_HEAD_DIM), F32)],
        compiler_params=pltpu.CompilerParams(
            dimension_semantics=("arbitrary", "arbitrary"), vmem_limit_bytes=VMEM_LIMIT),
        name="mla_attention",
    )(qp, kp, v)


def _layer_out_kernel(x_ref, y_ref, g_ref, gate_ref, wout_ref, gpost_ref, o_ref):
    aw = y_ref.shape[1]
    ya = (y_ref[...].astype(F32) * g_ref[:, 0:aw].astype(F32)).astype(BF16)
    y = (jnp.dot(ya, wout_ref[0:aw, :], preferred_element_type=F32)
         + jnp.dot(g_ref[:, aw:], wout_ref[aw:, :], preferred_element_type=F32))
    o_ref[...] = x_ref[...] + gate_ref[...] * _rms(y, gpost_ref[...])


def _layer_out(l, x, y_att, gated, mod, w_out_b, g_post, tm):
    B, S, D = x.shape
    seq = lambda w: pl.BlockSpec((None, tm, w), lambda b, i: (b, i, 0))
    return pl.pallas_call(
        _layer_out_kernel,
        grid=(B, S // tm),
        in_specs=[seq(D), seq(y_att.shape[-1]), seq(gated.shape[-1]), _mod_spec(l, 2, D),
                  _layer_spec(l, w_out_b), _layer_spec(l, g_post)],
        out_specs=seq(D),
        out_shape=jax.ShapeDtypeStruct((B, S, D), F32),
        compiler_params=pltpu.CompilerParams(
            dimension_semantics=("arbitrary", "arbitrary"), vmem_limit_bytes=VMEM_LIMIT),
        name="layer_out",
    )(x, y_att, gated, mod, w_out_b, g_post)


def _prep_w_in(w):
    pad = jnp.zeros(w.shape[:-1] + (64,), w.dtype)
    return jnp.concatenate([w[..., :384], w[..., 448:], w[..., 384:448], pad], axis=-1).astype(BF16)


def _prep_w_uq(w):
    lead = w.shape[:-1]
    w = w.reshape(lead + (ATT_HEADS, QK_NOPE_DIM + QK_ROPE_DIM))
    pad = jnp.zeros(lead + (ATT_HEADS, HEAD_PAD - QK_NOPE_DIM - QK_ROPE_DIM), w.dtype)
    return jnp.concatenate([w, pad], axis=-1).reshape(lead + (ATT_HEADS * HEAD_PAD,)).astype(BF16)


def _prep_w_ukv(w):
    lead = w.shape[:-1]
    w = w.reshape(lead + (ATT_HEADS, QK_NOPE_DIM + V_HEAD_DIM))
    return jnp.concatenate([w[..., :QK_NOPE_DIM].reshape(lead + (-1,)),
                            w[..., QK_NOPE_DIM:].reshape(lead + (-1,))], axis=-1).astype(BF16)


def kernel(x, c, positions, w_ada, b_ada, g_pre, g_post, w_in, q_norm_g, w_uq, kv_norm_g, w_ukv, conv_w,
           conv_b, conv_ln_g, conv_ln_b, w_pw2, sgu_ln_g, sgu_ln_b, w_s, b_s, w_out):
    B, S, D = x.shape
    L = w_ada.shape[0]
    tm = min(S, LAYER_IN_TILE)

    c_pad = jnp.concatenate([c, jnp.zeros((8 - B % 8, D), c.dtype)], axis=0) if B % 8 else c
    mod = _modulation(c_pad, w_ada, b_ada)
    mod = mod.reshape(L, mod.shape[1], 3, 1, D)
    tcos, tsin = _rope_tables(positions)

    vec = lambda a: a.reshape(L, 1, -1)
    conv_w_p = jnp.concatenate([conv_w, jnp.zeros((L, 1, conv_w.shape[-1]), F32)], axis=1)
    bs_wide = jnp.repeat(jnp.swapaxes(b_s, 1, 2), w_pw2.shape[-1] // SGU_GROUPS, axis=2)
    params = (vec(g_pre), _prep_w_in(w_in), vec(q_norm_g), _prep_w_uq(w_uq), vec(kv_norm_g),
              _prep_w_ukv(w_ukv), conv_w_p, vec(conv_b), vec(conv_ln_g), vec(conv_ln_b),
              w_pw2.astype(BF16), vec(sgu_ln_g), vec(sgu_ln_b), w_s, bs_wide)
    w_out_b = w_out.astype(BF16)
    g_post = vec(g_post)

    for l in range(L):
        qp, kp, v, gated = _layer_in(l, x, mod, tcos, tsin, params, tm)
        y_att = _attention(qp, kp, v)
        x = _layer_out(l, x, y_att, gated, mod, w_out_b, g_post, min(S, LAYER_OUT_TILE))
    return x
```

```python
import functools

import numpy as np
import jax
import jax.numpy as jnp
from jax import lax
from jax.experimental import pallas as pl
from jax.experimental.pallas import tpu as pltpu

F32 = jnp.float32
BF16 = jnp.bfloat16

NORM_EPS = 1e-6
ATT_HEADS = 4
QK_NOPE_DIM = 128
QK_ROPE_DIM = 64
V_HEAD_DIM = 128
HEAD_PAD = 256
V_PAD = 256
ROPE_THETA = 10000.0
CONV_KERNEL = 31
CONV_HALO = 32
CONV_ROW_BLOCK = 64
SGU_GROUPS = 4
SGU_CHUNK = 128
SQRT_HALF = float(np.sqrt(0.5).astype(np.float32))
ATT_SCALE = float((QK_NOPE_DIM + QK_ROPE_DIM) ** -0.5)
LOG2E = float(np.log2(np.e))
Q_SCALE = ATT_SCALE * LOG2E
VMEM_LIMIT = 56 * 1024 * 1024
LAYER_IN_TILE = 1024
LAYER_OUT_TILE = 1024


def _rms(x, g):
    return x * lax.rsqrt(jnp.mean(x * x, axis=-1, keepdims=True) + NORM_EPS) * g


def _layer_norm(x, g, b):
    mu = jnp.mean(x, axis=-1, keepdims=True)
    xc = x - mu
    var = jnp.mean(xc * xc, axis=-1, keepdims=True)
    return xc * lax.rsqrt(var + NORM_EPS) * g + b


def _sigmoid(x):
    return 1.0 / (1.0 + jnp.exp(-x))


def _silu(x):
    return x * _sigmoid(x)


def _gelu(x):
    return 0.5 * x * (1.0 + lax.erf(x * SQRT_HALF))


def _rope(t, tc, ts):
    lane = lax.broadcasted_iota(jnp.int32, t.shape, 1)
    first = lane < (QK_ROPE_DIM // 2)
    sw = jnp.where(first, pltpu.roll(t, 128 - QK_ROPE_DIM // 2, 1), pltpu.roll(t, QK_ROPE_DIM // 2, 1))
    return t * tc + jnp.where(first, -sw, sw) * ts


def _mod_kernel(c_ref, w_ref, b_ref, o_ref):
    c = c_ref[...]
    o_ref[...] = jnp.dot(_silu(c), w_ref[...], preferred_element_type=F32) + b_ref[...]


def _modulation(c_pad, w_ada, b_ada):
    L, D, D3 = w_ada.shape
    tn = 1024
    return pl.pallas_call(
        _mod_kernel,
        grid=(L, D3 // tn),
        in_specs=[
            pl.BlockSpec(c_pad.shape, lambda l, j: (0, 0)),
            pl.BlockSpec((None, D, tn), lambda l, j: (l, 0, j)),
            pl.BlockSpec((None, 1, tn), lambda l, j: (l, 0, j)),
        ],
        out_specs=pl.BlockSpec((None, c_pad.shape[0], tn), lambda l, j: (l, 0, j)),
        out_shape=jax.ShapeDtypeStruct((L, c_pad.shape[0], D3), F32),
        compiler_params=pltpu.CompilerParams(
            dimension_semantics=("arbitrary", "arbitrary"), vmem_limit_bytes=VMEM_LIMIT),
        name="adaln_mod",
    )(c_pad, w_ada, b_ada.reshape(L, 1, D3))


def _rope_table_kernel(pos_ref, invf_ref, cos_ref, sin_ref):
    half = QK_ROPE_DIM // 2
    per_row = 128 // half
    tr = pos_ref.shape[0]
    ang = pos_ref[...].astype(F32) * invf_ref[...]
    lane = lax.broadcasted_iota(jnp.int32, ang.shape, 1)
    for tab, out_ref in ((jnp.cos(ang), cos_ref), (jnp.sin(ang), sin_ref)):
        for k in range(per_row):
            lo = tab if k == 0 else pltpu.roll(tab, 128 - half * k, 1)
            hi = pltpu.roll(lo, half, 1)
            wide = jnp.where(lane < half, lo, jnp.where(lane < 2 * half, hi, 0.0))
            out_ref[pl.ds(k, tr, stride=per_row), :] = wide


def _rope_tables(positions):
    B, S = positions.shape
    half = QK_ROPE_DIM // 2
    per_row = 128 // half
    n_rows = B * S // per_row
    inv_freq = ROPE_THETA ** (-jnp.arange(0, QK_ROPE_DIM, 2, dtype=F32) / QK_ROPE_DIM)
    pos_wide = jnp.broadcast_to(
        positions.reshape(n_rows, per_row, 1), (n_rows, per_row, half)).reshape(n_rows, 128)
    invf_wide = jnp.tile(inv_freq, per_row).reshape(1, 128)
    tr = min(n_rows, 512)
    cos_w, sin_w = pl.pallas_call(
        _rope_table_kernel,
        grid=(n_rows // tr,),
        in_specs=[pl.BlockSpec((tr, 128), lambda i: (i, 0)), pl.BlockSpec((1, 128), lambda i: (0, 0))],
        out_specs=[pl.BlockSpec((tr * per_row, 128), lambda i: (i, 0))] * 2,
        out_shape=[jax.ShapeDtypeStruct((B * S, 128), F32)] * 2,
        compiler_params=pltpu.CompilerParams(dimension_semantics=("arbitrary",)),
        name="rope_table",
    )(pos_wide, invf_wide)
    return cos_w.reshape(B, S, 128), sin_w.reshape(B, S, 128)


_C_QLAT, _C_KVLAT, _C_GATT, _C_CA, _C_CB, _C_GC, _C_SU, _C_SV, _C_GS, _C_KR, _C_END = (
    0, 256, 384, 896, 1152, 1408, 1664, 1920, 2176, 2432, 2560)


def _layer_in_kernel(x_ref, shift_ref, scale_ref, gpre_ref, win_ref, qg_ref, wuq_ref, kvg_ref, wukv_ref,
                     tc_ref, ts_ref, cw_ref, cb_ref, clg_ref, clb_ref, wpw_ref,
                     slg_ref, slb_ref, ws_ref, bs_ref,
                     q_out, k_out, v_out, g_out, hbuf, shifted):
    tm = x_ref.shape[0]

    @pl.when(pl.program_id(1) == 0)
    def _():
        hbuf[0:CONV_HALO, :] = jnp.zeros((CONV_HALO, hbuf.shape[1]), F32)

    x = x_ref[...]
    h = _rms(x, gpre_ref[...]) * (1.0 + scale_ref[...]) + shift_ref[...]
    hb = h.astype(BF16)

    def proj(a, b):
        return jnp.dot(hb, win_ref[:, a:b], preferred_element_type=F32)

    tc = tc_ref[...]
    ts = ts_ref[...]

    glu = proj(_C_CA, _C_CB) * _sigmoid(proj(_C_CB, _C_GC))
    hbuf[CONV_HALO:CONV_HALO + tm, :] = glu
    base = CONV_HALO - (CONV_KERNEL - 1)
    z_q = proj(_C_QLAT, _C_KVLAT)
    z_kv = proj(_C_KVLAT, _C_GATT)
    z_kr = proj(_C_KR, _C_END)
    for b in range(1, 8):
        shifted[b - 1, :, :] = hbuf[pl.ds(b, shifted.shape[1]), :]

    def conv_taps(r0):
        acc = jnp.broadcast_to(cb_ref[...], (CONV_ROW_BLOCK, glu.shape[1]))
        for k in range(CONV_KERNEL):
            a, b = divmod(base + k, 8)
            rows = pl.ds(r0 + 8 * a, CONV_ROW_BLOCK)
            src = hbuf[rows, :] if b == 0 else shifted[b - 1, rows, :]
            acc = acc + cw_ref[k:k + 1, :] * src
        return acc

    n_blk = tm // CONV_ROW_BLOCK
    taps = lambda part: [conv_taps(r * CONV_ROW_BLOCK)
                         for r in range(part * n_blk // 4, (part + 1) * n_blk // 4)]
    qn = _rms(z_q, qg_ref[...]).astype(BF16)
    kvn = _rms(z_kv, kvg_ref[...]).astype(BF16)
    q = jnp.dot(qn, wuq_ref[...], preferred_element_type=F32)
    kv = jnp.dot(kvn, wukv_ref[...], preferred_element_type=F32)
    blocks = taps(0) + taps(1)
    kr = _rope(z_kr, tc, ts).astype(BF16)
    for hd in range(ATT_HEADS):
        o = hd * HEAD_PAD
        q_out[:, o:o + QK_NOPE_DIM] = (q[:, o:o + QK_NOPE_DIM] * Q_SCALE).astype(BF16)
        q_out[:, o + QK_NOPE_DIM:o + HEAD_PAD] = (
            _rope(q[:, o + QK_NOPE_DIM:o + HEAD_PAD], tc, ts) * Q_SCALE).astype(BF16)
        k_out[:, o:o + QK_NOPE_DIM] = kv[:, hd * QK_NOPE_DIM:(hd + 1) * QK_NOPE_DIM].astype(BF16)
        k_out[:, o + QK_NOPE_DIM:o + HEAD_PAD] = kr
        v_out[:, hd * V_PAD:hd * V_PAD + V_HEAD_DIM] = kv[
            :, (ATT_HEADS + hd) * QK_NOPE_DIM:(ATT_HEADS + hd + 1) * QK_NOPE_DIM].astype(BF16)
        v_out[:, hd * V_PAD + V_HEAD_DIM:(hd + 1) * V_PAD] = jnp.ones((tm, V_PAD - V_HEAD_DIM), BF16)

    g_out[:, 0:512] = _silu(proj(_C_GATT, _C_CA)).astype(BF16)
    blocks += taps(2)

    vn = _layer_norm(_gelu(proj(_C_SV, _C_GS)), slg_ref[...], slb_ref[...]).astype(BF16)
    s_gc = _silu(proj(_C_GC, _C_SU))
    blocks += taps(3)
    hbuf[0:CONV_HALO, :] = hbuf[tm:tm + CONV_HALO, :]
    z_su = proj(_C_SU, _C_SV)
    z_gs = proj(_C_GS, _C_KR)

    hc = _silu(_layer_norm(jnp.concatenate(blocks, axis=0), clg_ref[...], clb_ref[...])).astype(BF16)
    yc = jnp.dot(hc, wpw_ref[...], preferred_element_type=F32) * s_gc
    g_out[:, 512:768] = yc.astype(BF16)

    r = lax.broadcasted_iota(jnp.int32, (SGU_CHUNK, SGU_CHUNK), 0)
    c = lax.broadcasted_iota(jnp.int32, (SGU_CHUNK, SGU_CHUNK), 1)
    w_stack = jnp.concatenate(
        [jnp.where(c <= r, ws_ref[g], 0.0) for g in range(SGU_GROUPS)], axis=0).astype(BF16)
    gw = vn.shape[1] // SGU_GROUPS
    lane_grp = lax.broadcasted_iota(jnp.int32, (SGU_CHUNK, vn.shape[1]), 1) // gw
    svs = []
    for ck in range(tm // SGU_CHUNK):
        full = jnp.dot(w_stack, vn[ck * SGU_CHUNK:(ck + 1) * SGU_CHUNK, :], preferred_element_type=F32)
        sv = full[0:SGU_CHUNK]
        for g in range(1, SGU_GROUPS):
            sv = jnp.where(lane_grp == g, full[g * SGU_CHUNK:(g + 1) * SGU_CHUNK], sv)
        svs.append(sv + bs_ref[...])
    sv = jnp.concatenate(svs, axis=0)
    ys = _gelu(z_su) * sv * _silu(z_gs)
    g_out[:, 768:1024] = ys.astype(BF16)


def _mod_spec(l, part, D):
    return pl.BlockSpec((None, None, None, 1, D), lambda b, i: (l, b, part, 0, 0))


def _layer_spec(l, a):
    return pl.BlockSpec((None,) + a.shape[1:], lambda b, i: (l,) + (0,) * (a.ndim - 1))


def _layer_in(l, x, mod, tcos, tsin, params, tm):
    B, S, D = x.shape
    cw = params[7].shape[-1]
    seq = lambda w: pl.BlockSpec((None, tm, w), lambda b, i: (b, i, 0))
    args = [x, mod, mod] + list(params[:6]) + [tcos, tsin] + list(params[6:])
    in_specs = ([seq(D), _mod_spec(l, 0, D), _mod_spec(l, 1, D)] + [_layer_spec(l, a) for a in params[:6]]
                + [seq(128), seq(128)] + [_layer_spec(l, a) for a in params[6:]])
    hp = ATT_HEADS * HEAD_PAD
    return pl.pallas_call(
        _layer_in_kernel,
        grid=(B, S // tm),
        in_specs=in_specs,
        out_specs=[seq(hp), seq(hp), seq(ATT_HEADS * V_PAD), seq(D)],
        out_shape=[jax.ShapeDtypeStruct((B, S, hp), BF16), jax.ShapeDtypeStruct((B, S, hp), BF16),
                   jax.ShapeDtypeStruct((B, S, ATT_HEADS * V_PAD), BF16),
                   jax.ShapeDtypeStruct((B, S, D), BF16)],
        scratch_shapes=[pltpu.VMEM((CONV_HALO + tm, cw), F32),
                        pltpu.VMEM((7, CONV_HALO + tm - 8, cw), F32)],
        compiler_params=pltpu.CompilerParams(
            dimension_semantics=("arbitrary", "arbitrary"), vmem_limit_bytes=VMEM_LIMIT),
        name="layer_in",
    )(*args)


ATT_TILE = 512
ATT_ROW_BLOCK = 16


def _attn_kernel(q_ref, k_ref, v_ref, o_ref, s_a, s_b, s_c, p_a, p_b, p_c, al_a, al_b, al_c, m_st, acc_st):
    t = s_a.shape[0]
    nq = q_ref.shape[0] // t
    lanes = m_st.shape[2]
    n_col = t // lanes
    s_ring = (s_a, s_b, s_c)
    p_ring = ((p_a, al_a), (p_b, al_b), (p_c, al_c))

    def tile_rows(idx):
        if isinstance(idx, int):
            return pl.ds(idx * t, t)
        return pl.ds(pl.multiple_of(idx * t, t), t)

    def score_tile(s_ref, i, j):
        s_ref[...] = lax.dot_general(q_ref[tile_rows(i), :], k_ref[tile_rows(j), :],
                                     (((1,), (1,)), ((), ())), preferred_element_type=F32)

    def softmax_diag(s_ref, p_buf, i):
        p_ref, _ = p_buf
        for r0 in range(0, t, ATT_ROW_BLOCK):
            rows = pl.ds(r0, ATT_ROW_BLOCK)
            live = [c for c in range(n_col) if c * lanes <= r0 + ATT_ROW_BLOCK - 1]
            s = {}
            for c in live:
                sc = s_ref[rows, c * lanes:(c + 1) * lanes]
                if (c + 1) * lanes - 1 > r0:
                    row = r0 + lax.broadcasted_iota(jnp.int32, (ATT_ROW_BLOCK, lanes), 0)
                    col = c * lanes + lax.broadcasted_iota(jnp.int32, (ATT_ROW_BLOCK, lanes), 1)
                    sc = jnp.where(col <= row, sc, -jnp.inf)
                s[c] = sc
            m_new = jnp.max(functools.reduce(jnp.maximum, s.values()), axis=-1, keepdims=True)
            p = {c: jnp.exp2(s[c] - m_new) for c in live}
            m_st[i, rows, :] = jnp.broadcast_to(m_new, (ATT_ROW_BLOCK, lanes))
            for c in range(n_col):
                p_ref[rows, c * lanes:(c + 1) * lanes] = (
                    p[c].astype(BF16) if c in p else jnp.zeros((ATT_ROW_BLOCK, lanes), BF16))

    def softmax_lower(s_ref, p_buf, i):
        p_ref, al_ref = p_buf
        for r0 in range(0, t, ATT_ROW_BLOCK):
            rows = pl.ds(r0, ATT_ROW_BLOCK)
            s = [s_ref[rows, c * lanes:(c + 1) * lanes] for c in range(n_col)]
            m_old = m_st[i, rows, :]
            m_new = jnp.maximum(m_old, jnp.max(functools.reduce(jnp.maximum, s), axis=-1, keepdims=True))
            p = [jnp.exp2(sc - m_new) for sc in s]
            alpha = jnp.exp2(m_old - m_new)
            m_st[i, rows, :] = m_new
            al_ref[rows, :] = alpha
            for c in range(n_col):
                p_ref[rows, c * lanes:(c + 1) * lanes] = p[c].astype(BF16)

    def value_first(p_buf, i, j):
        acc_st[i] = jnp.dot(p_buf[0][...], v_ref[tile_rows(j), :], preferred_element_type=F32)

    def value_accumulate(p_buf, i, j):
        al = p_buf[1][...]
        acc_st[i] = jnp.concatenate([al, al], axis=1) * acc_st[i] + jnp.dot(
            p_buf[0][...], v_ref[tile_rows(j), :], preferred_element_type=F32)

    if nq == 1:
        score_tile(s_ring[0], 0, 0)
        softmax_diag(s_ring[0], p_ring[0], 0)
        value_first(p_ring[0], 0, 0)
    else:
        u1 = 4 if nq % 4 == 0 else 2
        assert nq % u1 == 0
        p_b[...] = jnp.zeros(p_b.shape, BF16)
        score_tile(s_ring[0], 0, 0)

        def diag_steps(it, carry):
            for d in range(u1):
                i = it * u1 + d
                cur, other = d % 2, (d + 1) % 2
                nxt = jnp.minimum(i + 1, nq - 1)
                prv = jnp.maximum(i - 1, 0)
                score_tile(s_ring[other], nxt, nxt)
                softmax_diag(s_ring[cur], p_ring[cur], i)
                value_first(p_ring[other], prv, prv)
            return carry

        lax.fori_loop(0, nq // u1, diag_steps, 0)
        value_first(p_ring[(nq - 1) % 2], nq - 1, nq - 1)

    order = [(i, j) for i in range(1, nq) for j in range(i)]
    n_steps = len(order)

    def lower_step(n, cur_ij, prev_ij, next_ij):
        score_tile(s_ring[(n + 1) % 3], *next_ij)
        softmax_lower(s_ring[n % 3], p_ring[n % 3], cur_ij[0])
        value_accumulate(p_ring[(n - 1) % 3], *prev_ij)

    if n_steps > 0:
        p_c[...] = jnp.zeros(p_c.shape, BF16)
        al_c[...] = jnp.ones(al_c.shape, F32)
        score_tile(s_ring[0], 1, 0)

    unrolls = [u for u in (12, 6, 3) if n_steps % u == 0]
    if not unrolls:
        assert n_steps <= 32
        for n, ij in enumerate(order):
            lower_step(n, ij, order[max(n - 1, 0)], order[min(n + 1, n_steps - 1)])
        value_accumulate(p_ring[(n_steps - 1) % 3], *order[-1])
    elif n_steps > 0:
        u2 = unrolls[0]

        def advance(i, j):
            wrap = (j + 1) == i
            return jnp.where(wrap, i + 1, i), jnp.where(wrap, 0, j + 1)

        def softmax_lower_values(s, i):
            p_rows, al_rows = [], []
            for r0 in range(0, t, ATT_ROW_BLOCK):
                rows = pl.ds(r0, ATT_ROW_BLOCK)
                sb = [s[r0:r0 + ATT_ROW_BLOCK, c * lanes:(c + 1) * lanes] for c in range(n_col)]
                m_old = m_st[i, rows, :]
                m_new = jnp.maximum(
                    m_old, jnp.max(functools.reduce(jnp.maximum, sb), axis=-1, keepdims=True))
                p = [jnp.exp2(sc - m_new) for sc in sb]
                alpha = jnp.exp2(m_old - m_new)
                m_st[i, rows, :] = m_new
                p_rows.append(jnp.concatenate([pc.astype(BF16) for pc in p], axis=1))
                al_rows.append(alpha)
            return jnp.concatenate(p_rows, axis=0), jnp.concatenate(al_rows, axis=0)

        def lower_steps(it, carry):
            i, j, ip, jp = carry
            s_cur, p_prev, al_prev = s_a[...], p_c[...], al_c[...]
            for _ in range(u2):
                i_n, j_n = advance(i, j)
                s_nxt = lax.dot_general(q_ref[tile_rows(jnp.minimum(i_n, nq - 1)), :], k_ref[tile_rows(j_n), :],
                                        (((1,), (1,)), ((), ())), preferred_element_type=F32)
                p_cur, al_cur = softmax_lower_values(s_cur, i)
                acc_st[ip] = jnp.concatenate([al_prev, al_prev], axis=1) * acc_st[ip] + jnp.dot(
                    p_prev, v_ref[tile_rows(jp), :], preferred_element_type=F32)
                s_cur, p_prev, al_prev = s_nxt, p_cur, al_cur
                i, j, ip, jp = i_n, j_n, i, j
            s_a[...] = s_cur
            p_c[...] = p_prev
            al_c[...] = al_prev
            return i, j, ip, jp

        one = jnp.int32(1)
        zero = jnp.int32(0)
        _, _, ip, jp = lax.fori_loop(0, n_steps // u2, lower_steps, (one, zero, one, zero))
        value_accumulate(p_ring[2], ip, jp)

    def finish(i, carry):
        acc = acc_st[i]
        o_ref[tile_rows(i), :] = (acc[:, :lanes] / acc[:, lanes:]).astype(o_ref.dtype)
        return carry

    lax.fori_loop(0, nq, finish, 0)


def _attention(qp, kp, v):
    B, S, _ = qp.shape
    t = min(S, ATT_TILE)
    nq = S // t
    lanes = V_HEAD_DIM
    head_block = lambda w: pl.BlockSpec((None, S, w), lambda b, h: (b, 0, h))
    return pl.pallas_call(
        _attn_kernel,
        grid=(B, ATT_HEADS),
        in_specs=[head_block(HEAD_PAD), head_block(HEAD_PAD), head_block(V_PAD)],
        out_specs=head_block(V_HEAD_DIM),
        out_shape=jax.ShapeDtypeStruct((B, S, ATT_HEADS * V_HEAD_DIM), BF16),
        scratch_shapes=[pltpu.VMEM((t, t), F32)] * 3
                       + [pltpu.VMEM((t, t), BF16)] * 3
                       + [pltpu.VMEM((t, lanes), F32)] * 3
                       + [
                        pltpu.VMEM((nq, t, lanes), F32),
                        pltpu.VMEM((nq, t, V_PAD), F32)],
        compiler_params=pltpu.CompilerParams(
            dimension_semantics=("arbitrary", "arbitrary"), vmem_limit_bytes=VMEM_LIMIT),
        name="mla_attention",
    )(qp, kp, v)


def _layer_out_kernel(x_ref, y_ref, g_ref, gate_ref, wout_ref, gpost_ref, o_ref):
    aw = y_ref.shape[1]
    ya = (y_ref[...].astype(F32) * g_ref[:, 0:aw].astype(F32)).astype(BF16)
    y = (jnp.dot(ya, wout_ref[0:aw, :], preferred_element_type=F32)
         + jnp.dot(g_ref[:, aw:], wout_ref[aw:, :], preferred_element_type=F32))
    o_ref[...] = x_ref[...] + gate_ref[...] * _rms(y, gpost_ref[...])


def _layer_out(l, x, y_att, gated, mod, w_out_b, g_post, tm):
    B, S, D = x.shape
    seq = lambda w: pl.BlockSpec((None, tm, w), lambda b, i: (b, i, 0))
    return pl.pallas_call(
        _layer_out_kernel,
        grid=(B, S // tm),
        in_specs=[seq(D), seq(y_att.shape[-1]), seq(gated.shape[-1]), _mod_spec(l, 2, D),
                  _layer_spec(l, w_out_b), _layer_spec(l, g_post)],
        out_specs=seq(D),
        out_shape=jax.ShapeDtypeStruct((B, S, D), F32),
        compiler_params=pltpu.CompilerParams(
            dimension_semantics=("arbitrary", "arbitrary"), vmem_limit_bytes=VMEM_LIMIT),
        name="layer_out",
    )(x, y_att, gated, mod, w_out_b, g_post)


def _prep_w_in(w):
    pad = jnp.zeros(w.shape[:-1] + (64,), w.dtype)
    return jnp.concatenate([w[..., :384], w[..., 448:], w[..., 384:448], pad], axis=-1).astype(BF16)


def _prep_w_uq(w):
    lead = w.shape[:-1]
    w = w.reshape(lead + (ATT_HEADS, QK_NOPE_DIM + QK_ROPE_DIM))
    pad = jnp.zeros(lead + (ATT_HEADS, HEAD_PAD - QK_NOPE_DIM - QK_ROPE_DIM), w.dtype)
    return jnp.concatenate([w, pad], axis=-1).reshape(lead + (ATT_HEADS * HEAD_PAD,)).astype(BF16)


def _prep_w_ukv(w):
    lead = w.shape[:-1]
    w = w.reshape(lead + (ATT_HEADS, QK_NOPE_DIM + V_HEAD_DIM))
    return jnp.concatenate([w[..., :QK_NOPE_DIM].reshape(lead + (-1,)),
                            w[..., QK_NOPE_DIM:].reshape(lead + (-1,))], axis=-1).astype(BF16)


def kernel(x, c, positions, w_ada, b_ada, g_pre, g_post, w_in, q_norm_g, w_uq, kv_norm_g, w_ukv, conv_w,
           conv_b, conv_ln_g, conv_ln_b, w_pw2, sgu_ln_g, sgu_ln_b, w_s, b_s, w_out):
    B, S, D = x.shape
    L = w_ada.shape[0]
    tm = min(S, LAYER_IN_TILE)

    c_pad = jnp.concatenate([c, jnp.zeros((8 - B % 8, D), c.dtype)], axis=0) if B % 8 else c
    mod = _modulation(c_pad, w_ada, b_ada)
    mod = mod.reshape(L, mod.shape[1], 3, 1, D)
    tcos, tsin = _rope_tables(positions)

    vec = lambda a: a.reshape(L, 1, -1)
    conv_w_p = jnp.concatenate([conv_w, jnp.zeros((L, 1, conv_w.shape[-1]), F32)], axis=1)
    bs_wide = jnp.repeat(jnp.swapaxes(b_s, 1, 2), w_pw2.shape[-1] // SGU_GROUPS, axis=2)
    params = (vec(g_pre), _prep_w_in(w_in), vec(q_norm_g), _prep_w_uq(w_uq), vec(kv_norm_g),
              _prep_w_ukv(w_ukv), conv_w_p, vec(conv_b), vec(conv_ln_g), vec(conv_ln_b),
              w_pw2.astype(BF16), vec(sgu_ln_g), vec(sgu_ln_b), w_s, bs_wide)
    w_out_b = w_out.astype(BF16)
    g_post = vec(g_post)

    for l in range(L):
        qp, kp, v, gated = _layer_in(l, x, mod, tcos, tsin, params, tm)
        y_att = _attention(qp, kp, v)
        x = _layer_out(l, x, y_att, gated, mod, w_out_b, g_post, min(S, LAYER_OUT_TILE))
    return x
```

```python
import functools

import numpy as np
import jax
import jax.numpy as jnp
from jax import lax
from jax.experimental import pallas as pl
from jax.experimental.pallas import tpu as pltpu

F32 = jnp.float32
BF16 = jnp.bfloat16

NORM_EPS = 1e-6
ATT_HEADS = 4
QK_NOPE_DIM = 128
QK_ROPE_DIM = 64
V_HEAD_DIM = 128
HEAD_PAD = 256
V_PAD = 256
ROPE_THETA = 10000.0
CONV_KERNEL = 31
CONV_HALO = 32
CONV_ROW_BLOCK = 64
SGU_GROUPS = 4
SGU_CHUNK = 128
SQRT_HALF = float(np.sqrt(0.5).astype(np.float32))
ATT_SCALE = float((QK_NOPE_DIM + QK_ROPE_DIM) ** -0.5)
LOG2E = float(np.log2(np.e))
Q_SCALE = ATT_SCALE * LOG2E
VMEM_LIMIT = 56 * 1024 * 1024
LAYER_IN_TILE = 1024
LAYER_OUT_TILE = 1024


def _rms(x, g):
    return x * lax.rsqrt(jnp.mean(x * x, axis=-1, keepdims=True) + NORM_EPS) * g


def _layer_norm(x, g, b):
    mu = jnp.mean(x, axis=-1, keepdims=True)
    xc = x - mu
    var = jnp.mean(xc * xc, axis=-1, keepdims=True)
    return xc * lax.rsqrt(var + NORM_EPS) * g + b


def _sigmoid(x):
    return 1.0 / (1.0 + jnp.exp(-x))


def _silu(x):
    return x * _sigmoid(x)


def _gelu(x):
    return 0.5 * x * (1.0 + lax.erf(x * SQRT_HALF))


def _rope(t, tc, ts):
    lane = lax.broadcasted_iota(jnp.int32, t.shape, 1)
    first = lane < (QK_ROPE_DIM // 2)
    sw = jnp.where(first, pltpu.roll(t, 128 - QK_ROPE_DIM // 2, 1), pltpu.roll(t, QK_ROPE_DIM // 2, 1))
    return t * tc + jnp.where(first, -sw, sw) * ts


def _mod_kernel(c_ref, w_ref, b_ref, o_ref):
    c = c_ref[...]
    o_ref[...] = jnp.dot(_silu(c), w_ref[...], preferred_element_type=F32) + b_ref[...]


def _modulation(c_pad, w_ada, b_ada):
    L, D, D3 = w_ada.shape
    tn = 1024
    return pl.pallas_call(
        _mod_kernel,
        grid=(L, D3 // tn),
        in_specs=[
            pl.BlockSpec(c_pad.shape, lambda l, j: (0, 0)),
            pl.BlockSpec((None, D, tn), lambda l, j: (l, 0, j)),
            pl.BlockSpec((None, 1, tn), lambda l, j: (l, 0, j)),
        ],
        out_specs=pl.BlockSpec((None, c_pad.shape[0], tn), lambda l, j: (l, 0, j)),
        out_shape=jax.ShapeDtypeStruct((L, c_pad.shape[0], D3), F32),
        compiler_params=pltpu.CompilerParams(
            dimension_semantics=("arbitrary", "arbitrary"), vmem_limit_bytes=VMEM_LIMIT),
        name="adaln_mod",
    )(c_pad, w_ada, b_ada.reshape(L, 1, D3))


def _rope_table_kernel(pos_ref, invf_ref, cos_ref, sin_ref):
    half = QK_ROPE_DIM // 2
    per_row = 128 // half
    tr = pos_ref.shape[0]
    ang = pos_ref[...].astype(F32) * invf_ref[...]
    lane = lax.broadcasted_iota(jnp.int32, ang.shape, 1)
    for tab, out_ref in ((jnp.cos(ang), cos_ref), (jnp.sin(ang), sin_ref)):
        for k in range(per_row):
            lo = tab if k == 0 else pltpu.roll(tab, 128 - half * k, 1)
            hi = pltpu.roll(lo, half, 1)
            wide = jnp.where(lane < half, lo, jnp.where(lane < 2 * half, hi, 0.0))
            out_ref[pl.ds(k, tr, stride=per_row), :] = wide


def _rope_tables(positions):
    B, S = positions.shape
    half = QK_ROPE_DIM // 2
    per_row = 128 // half
    n_rows = B * S // per_row
    inv_freq = ROPE_THETA ** (-jnp.arange(0, QK_ROPE_DIM, 2, dtype=F32) / QK_ROPE_DIM)
    pos_wide = jnp.broadcast_to(
        positions.reshape(n_rows, per_row, 1), (n_rows, per_row, half)).reshape(n_rows, 128)
    invf_wide = jnp.tile(inv_freq, per_row).reshape(1, 128)
    tr = min(n_rows, 512)
    cos_w, sin_w = pl.pallas_call(
        _rope_table_kernel,
        grid=(n_rows // tr,),
        in_specs=[pl.BlockSpec((tr, 128), lambda i: (i, 0)), pl.BlockSpec((1, 128), lambda i: (0, 0))],
        out_specs=[pl.BlockSpec((tr * per_row, 128), lambda i: (i, 0))] * 2,
        out_shape=[jax.ShapeDtypeStruct((B * S, 128), F32)] * 2,
        compiler_params=pltpu.CompilerParams(dimension_semantics=("arbitrary",)),
        name="rope_table",
    )(pos_wide, invf_wide)
    return cos_w.reshape(B, S, 128), sin_w.reshape(B, S, 128)


_C_QLAT, _C_KVLAT, _C_GATT, _C_CA, _C_CB, _C_GC, _C_SU, _C_SV, _C_GS, _C_KR, _C_END = (
    0, 256, 384, 896, 1152, 1408, 1664, 1920, 2176, 2432, 2560)


def _layer_in_kernel(x_ref, shift_ref, scale_ref, gpre_ref, win_ref, qg_ref, wuq_ref, kvg_ref, wukv_ref,
                     tc_ref, ts_ref, cw_ref, cb_ref, clg_ref, clb_ref, wpw_ref,
                     slg_ref, slb_ref, ws_ref, bs_ref,
                     q_out, k_out, v_out, g_out, hbuf, shifted):
    tm = x_ref.shape[0]

    @pl.when(pl.program_id(1) == 0)
    def _():
        hbuf[0:CONV_HALO, :] = jnp.zeros((CONV_HALO, hbuf.shape[1]), F32)

    x = x_ref[...]
    h = _rms(x, gpre_ref[...]) * (1.0 + scale_ref[...]) + shift_ref[...]
    hb = h.astype(BF16)

    def proj(a, b):
        return jnp.dot(hb, win_ref[:, a:b], preferred_element_type=F32)

    tc = tc_ref[...]
    ts = ts_ref[...]

    glu = proj(_C_CA, _C_CB) * _sigmoid(proj(_C_CB, _C_GC))
    hbuf[CONV_HALO:CONV_HALO + tm, :] = glu
    base = CONV_HALO - (CONV_KERNEL - 1)
    z_q = proj(_C_QLAT, _C_KVLAT)
    z_kv = proj(_C_KVLAT, _C_GATT)
    z_kr = proj(_C_KR, _C_END)
    for b in range(1, 8):
        shifted[b - 1, :, :] = hbuf[pl.ds(b, shifted.shape[1]), :]

    def conv_taps(r0):
        acc = jnp.broadcast_to(cb_ref[...], (CONV_ROW_BLOCK, glu.shape[1]))
        for k in range(CONV_KERNEL):
            a, b = divmod(base + k, 8)
            rows = pl.ds(r0 + 8 * a, CONV_ROW_BLOCK)
            src = hbuf[rows, :] if b == 0 else shifted[b - 1, rows, :]
            acc = acc + cw_ref[k:k + 1, :] * src
        return acc

    n_blk = tm // CONV_ROW_BLOCK
    taps = lambda part: [conv_taps(r * CONV_ROW_BLOCK)
                         for r in range(part * n_blk // 4, (part + 1) * n_blk // 4)]
    qn = _rms(z_q, qg_ref[...]).astype(BF16)
    kvn = _rms(z_kv, kvg_ref[...]).astype(BF16)
    q = jnp.dot(qn, wuq_ref[...], preferred_element_type=F32)
    kv = jnp.dot(kvn, wukv_ref[...], preferred_element_type=F32)
    blocks = taps(0) + taps(1)
    kr = _rope(z_kr, tc, ts).astype(BF16)
    for hd in range(ATT_HEADS):
        o = hd * HEAD_PAD
        q_out[:, o:o + QK_NOPE_DIM] = (q[:, o:o + QK_NOPE_DIM] * Q_SCALE).astype(BF16)
        q_out[:, o + QK_NOPE_DIM:o + HEAD_PAD] = (
            _rope(q[:, o + QK_NOPE_DIM:o + HEAD_PAD], tc, ts) * Q_SCALE).astype(BF16)
        k_out[:, o:o + QK_NOPE_DIM] = kv[:, hd * QK_NOPE_DIM:(hd + 1) * QK_NOPE_DIM].astype(BF16)
        k_out[:, o + QK_NOPE_DIM:o + HEAD_PAD] = kr
        v_out[:, hd * V_PAD:hd * V_PAD + V_HEAD_DIM] = kv[
            :, (ATT_HEADS + hd) * QK_NOPE_DIM:(ATT_HEADS + hd + 1) * QK_NOPE_DIM].astype(BF16)
        v_out[:, hd * V_PAD + V_HEAD_DIM:(hd + 1) * V_PAD] = jnp.ones((tm, V_PAD - V_HEAD_DIM), BF16)

    g_out[:, 0:512] = _silu(proj(_C_GATT, _C_CA)).astype(BF16)
    blocks += taps(2)

    vn = _layer_norm(_gelu(proj(_C_SV, _C_GS)), slg_ref[...], slb_ref[...]).astype(BF16)
    s_gc = _silu(proj(_C_GC, _C_SU))
    blocks += taps(3)
    hbuf[0:CONV_HALO, :] = hbuf[tm:tm + CONV_HALO, :]
    z_su = proj(_C_SU, _C_SV)
    z_gs = proj(_C_GS, _C_KR)

    hc = _silu(_layer_norm(jnp.concatenate(blocks, axis=0), clg_ref[...], clb_ref[...])).astype(BF16)
    yc = jnp.dot(hc, wpw_ref[...], preferred_element_type=F32) * s_gc
    g_out[:, 512:768] = yc.astype(BF16)

    r = lax.broadcasted_iota(jnp.int32, (SGU_CHUNK, SGU_CHUNK), 0)
    c = lax.broadcasted_iota(jnp.int32, (SGU_CHUNK, SGU_CHUNK), 1)
    w_stack = jnp.concatenate(
        [jnp.where(c <= r, ws_ref[g], 0.0) for g in range(SGU_GROUPS)], axis=0).astype(BF16)
    gw = vn.shape[1] // SGU_GROUPS
    lane_grp = lax.broadcasted_iota(jnp.int32, (SGU_CHUNK, vn.shape[1]), 1) // gw
    svs = []
    for ck in range(tm // SGU_CHUNK):
        full = jnp.dot(w_stack, vn[ck * SGU_CHUNK:(ck + 1) * SGU_CHUNK, :], preferred_element_type=F32)
        sv = full[0:SGU_CHUNK]
        for g in range(1, SGU_GROUPS):
            sv = jnp.where(lane_grp == g, full[g * SGU_CHUNK:(g + 1) * SGU_CHUNK], sv)
        svs.append(sv + bs_ref[...])
    sv = jnp.concatenate(svs, axis=0)
    ys = _gelu(z_su) * sv * _silu(z_gs)
    g_out[:, 768:1024] = ys.astype(BF16)


def _mod_spec(l, part, D):
    return pl.BlockSpec((None, None, None, 1, D), lambda b, i: (l, b, part, 0, 0))


def _layer_spec(l, a):
    return pl.BlockSpec((None,) + a.shape[1:], lambda b, i: (l,) + (0,) * (a.ndim - 1))


def _layer_in(l, x, mod, tcos, tsin, params, tm):
    B, S, D = x.shape
    cw = params[7].shape[-1]
    seq = lambda w: pl.BlockSpec((None, tm, w), lambda b, i: (b, i, 0))
    args = [x, mod, mod] + list(params[:6]) + [tcos, tsin] + list(params[6:])
    in_specs = ([seq(D), _mod_spec(l, 0, D), _mod_spec(l, 1, D)] + [_layer_spec(l, a) for a in params[:6]]
                + [seq(128), seq(128)] + [_layer_spec(l, a) for a in params[6:]])
    hp = ATT_HEADS * HEAD_PAD
    return pl.pallas_call(
        _layer_in_kernel,
        grid=(B, S // tm),
        in_specs=in_specs,
        out_specs=[seq(hp), seq(hp), seq(ATT_HEADS * V_PAD), seq(D)],
        out_shape=[jax.ShapeDtypeStruct((B, S, hp), BF16), jax.ShapeDtypeStruct((B, S, hp), BF16),
                   jax.ShapeDtypeStruct((B, S, ATT_HEADS * V_PAD), BF16),
                   jax.ShapeDtypeStruct((B, S, D), BF16)],
        scratch_shapes=[pltpu.VMEM((CONV_HALO + tm, cw), F32),
                        pltpu.VMEM((7, CONV_HALO + tm - 8, cw), F32)],
        compiler_params=pltpu.CompilerParams(
            dimension_semantics=("arbitrary", "arbitrary"), vmem_limit_bytes=VMEM_LIMIT),
        name="layer_in",
    )(*args)


ATT_TILE = 512
ATT_ROW_BLOCK = 16


def _attn_kernel(q_ref, k_ref, v_ref, o_ref, s_a, s_b, s_c, p_a, p_b, p_c, al_a, al_b, al_c, m_st, acc_st):
    t = s_a.shape[0]
    nq = q_ref.shape[0] // t
    lanes = m_st.shape[2]
    n_col = t // lanes
    s_ring = (s_a, s_b, s_c)
    p_ring = ((p_a, al_a), (p_b, al_b), (p_c, al_c))

    def tile_rows(idx):
        if isinstance(idx, int):
            return pl.ds(idx * t, t)
        return pl.ds(pl.multiple_of(idx * t, t), t)

    def score_tile(s_ref, i, j):
        s_ref[...] = lax.dot_general(q_ref[tile_rows(i), :], k_ref[tile_rows(j), :],
                                     (((1,), (1,)), ((), ())), preferred_element_type=F32)

    def softmax_diag(s_ref, p_buf, i):
        p_ref, _ = p_buf
        for r0 in range(0, t, ATT_ROW_BLOCK):
            rows = pl.ds(r0, ATT_ROW_BLOCK)
            live = [c for c in range(n_col) if c * lanes <= r0 + ATT_ROW_BLOCK - 1]
            s = {}
            for c in live:
                sc = s_ref[rows, c * lanes:(c + 1) * lanes]
                if (c + 1) * lanes - 1 > r0:
                    row = r0 + lax.broadcasted_iota(jnp.int32, (ATT_ROW_BLOCK, lanes), 0)
                    col = c * lanes + lax.broadcasted_iota(jnp.int32, (ATT_ROW_BLOCK, lanes), 1)
                    sc = jnp.where(col <= row, sc, -jnp.inf)
                s[c] = sc
            m_new = jnp.max(functools.reduce(jnp.maximum, s.values()), axis=-1, keepdims=True)
            p = {c: jnp.exp2(s[c] - m_new) for c in live}
            m_st[i, rows, :] = jnp.broadcast_to(m_new, (ATT_ROW_BLOCK, lanes))
            for c in range(n_col):
                p_ref[rows, c * lanes:(c + 1) * lanes] = (
                    p[c].astype(BF16) if c in p else jnp.zeros((ATT_ROW_BLOCK, lanes), BF16))

    def softmax_lower(s_ref, p_buf, i):
        p_ref, al_ref = p_buf
        for r0 in range(0, t, ATT_ROW_BLOCK):
            rows = pl.ds(r0, ATT_ROW_BLOCK)
            s = [s_ref[rows, c * lanes:(c + 1) * lanes] for c in range(n_col)]
            m_old = m_st[i, rows, :]
            m_new = jnp.maximum(m_old, jnp.max(functools.reduce(jnp.maximum, s), axis=-1, keepdims=True))
            p = [jnp.exp2(sc - m_new) for sc in s]
            alpha = jnp.exp2(m_old - m_new)
            m_st[i, rows, :] = m_new
            al_ref[rows, :] = alpha
            for c in range(n_col):
                p_ref[rows, c * lanes:(c + 1) * lanes] = p[c].astype(BF16)

    def value_first(p_buf, i, j):
        acc_st[i] = jnp.dot(p_buf[0][...], v_ref[tile_rows(j), :], preferred_element_type=F32)

    def value_accumulate(p_buf, i, j):
        al = p_buf[1][...]
        acc_st[i] = jnp.concatenate([al, al], axis=1) * acc_st[i] + jnp.dot(
            p_buf[0][...], v_ref[tile_rows(j), :], preferred_element_type=F32)

    if nq == 1:
        score_tile(s_ring[0], 0, 0)
        softmax_diag(s_ring[0], p_ring[0], 0)
        value_first(p_ring[0], 0, 0)
    else:
        u1 = 4 if nq % 4 == 0 else 2
        assert nq % u1 == 0
        p_b[...] = jnp.zeros(p_b.shape, BF16)
        score_tile(s_ring[0], 0, 0)

        def diag_steps(it, carry):
            for d in range(u1):
                i = it * u1 + d
                cur, other = d % 2, (d + 1) % 2
                nxt = jnp.minimum(i + 1, nq - 1)
                prv = jnp.maximum(i - 1, 0)
                score_tile(s_ring[other], nxt, nxt)
                softmax_diag(s_ring[cur], p_ring[cur], i)
                value_first(p_ring[other], prv, prv)
            return carry

        lax.fori_loop(0, nq // u1, diag_steps, 0)
        value_first(p_ring[(nq - 1) % 2], nq - 1, nq - 1)

    order = [(i, j) for i in range(1, nq) for j in range(i)]
    n_steps = len(order)

    def lower_step(n, cur_ij, prev_ij, next_ij):
        score_tile(s_ring[(n + 1) % 3], *next_ij)
        softmax_lower(s_ring[n % 3], p_ring[n % 3], cur_ij[0])
        value_accumulate(p_ring[(n - 1) % 3], *prev_ij)

    if n_steps > 0:
        p_c[...] = jnp.zeros(p_c.shape, BF16)
        al_c[...] = jnp.ones(al_c.shape, F32)
        score_tile(s_ring[0], 1, 0)

    unrolls = [u for u in (24, 12, 6, 3) if n_steps % u == 0]
    if not unrolls:
        assert n_steps <= 32
        for n, ij in enumerate(order):
            lower_step(n, ij, order[max(n - 1, 0)], order[min(n + 1, n_steps - 1)])
        value_accumulate(p_ring[(n_steps - 1) % 3], *order[-1])
    elif n_steps > 0:
        u2 = unrolls[0]

        def advance(i, j):
            wrap = (j + 1) == i
            return jnp.where(wrap, i + 1, i), jnp.where(wrap, 0, j + 1)

        def softmax_lower_values(s, i):
            p_rows, al_rows = [], []
            for r0 in range(0, t, ATT_ROW_BLOCK):
                rows = pl.ds(r0, ATT_ROW_BLOCK)
                sb = [s[r0:r0 + ATT_ROW_BLOCK, c * lanes:(c + 1) * lanes] for c in range(n_col)]
                m_old = m_st[i, rows, :]
                m_new = jnp.maximum(
                    m_old, jnp.max(functools.reduce(jnp.maximum, sb), axis=-1, keepdims=True))
                p = [jnp.exp2(sc - m_new) for sc in sb]
                alpha = jnp.exp2(m_old - m_new)
                m_st[i, rows, :] = m_new
                p_rows.append(jnp.concatenate([pc.astype(BF16) for pc in p], axis=1))
                al_rows.append(alpha)
            return jnp.concatenate(p_rows, axis=0), jnp.concatenate(al_rows, axis=0)

        def lower_steps(it, carry):
            i, j, ip, jp = carry
            s_cur, p_prev, al_prev = s_a[...], p_c[...], al_c[...]
            for _ in range(u2):
                i_n, j_n = advance(i, j)
                s_nxt = lax.dot_general(q_ref[tile_rows(jnp.minimum(i_n, nq - 1)), :], k_ref[tile_rows(j_n), :],
                                        (((1,), (1,)), ((), ())), preferred_element_type=F32)
                p_cur, al_cur = softmax_lower_values(s_cur, i)
                acc_st[ip] = jnp.concatenate([al_prev, al_prev], axis=1) * acc_st[ip] + jnp.dot(
                    p_prev, v_ref[tile_rows(jp), :], preferred_element_type=F32)
                s_cur, p_prev, al_prev = s_nxt, p_cur, al_cur
                i, j, ip, jp = i_n, j_n, i, j
            s_a[...] = s_cur
            p_c[...] = p_prev
            al_c[...] = al_prev
            return i, j, ip, jp

        one = jnp.int32(1)
        zero = jnp.int32(0)
        _, _, ip, jp = lax.fori_loop(0, n_steps // u2, lower_steps, (one, zero, one, zero))
        value_accumulate(p_ring[2], ip, jp)

    def finish(i, carry):
        acc = acc_st[i]
        o_ref[tile_rows(i), :] = (acc[:, :lanes] / acc[:, lanes:]).astype(o_ref.dtype)
        return carry

    lax.fori_loop(0, nq, finish, 0)


def _attention(qp, kp, v):
    B, S, _ = qp.shape
    t = min(S, ATT_TILE)
    nq = S // t
    lanes = V_HEAD_DIM
    head_block = lambda w: pl.BlockSpec((None, S, w), lambda b, h: (b, 0, h))
    return pl.pallas_call(
        _attn_kernel,
        grid=(B, ATT_HEADS),
        in_specs=[head_block(HEAD_PAD), head_block(HEAD_PAD), head_block(V_PAD)],
        out_specs=head_block(V_HEAD_DIM),
        out_shape=jax.ShapeDtypeStruct((B, S, ATT_HEADS * V_HEAD_DIM), BF16),
        scratch_shapes=[pltpu.VMEM((t, t), F32)] * 3
                       + [pltpu.VMEM((t, t), BF16)] * 3
                       + [pltpu.VMEM((t, lanes), F32)] * 3
                       + [
                        pltpu.VMEM((nq, t, lanes), F32),
                        pltpu.VMEM((nq, t, V_PAD), F32)],
        compiler_params=pltpu.CompilerParams(
            dimension_semantics=("arbitrary", "arbitrary"), vmem_limit_bytes=VMEM_LIMIT),
        name="mla_attention",
    )(qp, kp, v)


def _layer_out_kernel(x_ref, y_ref, g_ref, gate_ref, wout_ref, gpost_ref, o_ref):
    aw = y_ref.shape[1]
    ya = (y_ref[...].astype(F32) * g_ref[:, 0:aw].astype(F32)).astype(BF16)
    y = (jnp.dot(ya, wout_ref[0:aw, :], preferred_element_type=F32)
         + jnp.dot(g_ref[:, aw:], wout_ref[aw:, :], preferred_element_type=F32))
    o_ref[...] = x_ref[...] + gate_ref[...] * _rms(y, gpost_ref[...])


def _layer_out(l, x, y_att, gated, mod, w_out_b, g_post, tm):
    B, S, D = x.shape
    seq = lambda w: pl.BlockSpec((None, tm, w), lambda b, i: (b, i, 0))
    return pl.pallas_call(
        _layer_out_kernel,
        grid=(B, S // tm),
        in_specs=[seq(D), seq(y_att.shape[-1]), seq(gated.shape[-1]), _mod_spec(l, 2, D),
                  _layer_spec(l, w_out_b), _layer_spec(l, g_post)],
        out_specs=seq(D),
        out_shape=jax.ShapeDtypeStruct((B, S, D), F32),
        compiler_params=pltpu.CompilerParams(
            dimension_semantics=("arbitrary", "arbitrary"), vmem_limit_bytes=VMEM_LIMIT),
        name="layer_out",
    )(x, y_att, gated, mod, w_out_b, g_post)


def _prep_w_in(w):
    pad = jnp.zeros(w.shape[:-1] + (64,), w.dtype)
    return jnp.concatenate([w[..., :384], w[..., 448:], w[..., 384:448], pad], axis=-1).astype(BF16)


def _prep_w_uq(w):
    lead = w.shape[:-1]
    w = w.reshape(lead + (ATT_HEADS, QK_NOPE_DIM + QK_ROPE_DIM))
    pad = jnp.zeros(lead + (ATT_HEADS, HEAD_PAD - QK_NOPE_DIM - QK_ROPE_DIM), w.dtype)
    return jnp.concatenate([w, pad], axis=-1).reshape(lead + (ATT_HEADS * HEAD_PAD,)).astype(BF16)


def _prep_w_ukv(w):
    lead = w.shape[:-1]
    w = w.reshape(lead + (ATT_HEADS, QK_NOPE_DIM + V_HEAD_DIM))
    return jnp.concatenate([w[..., :QK_NOPE_DIM].reshape(lead + (-1,)),
                            w[..., QK_NOPE_DIM:].reshape(lead + (-1,))], axis=-1).astype(BF16)


def kernel(x, c, positions, w_ada, b_ada, g_pre, g_post, w_in, q_norm_g, w_uq, kv_norm_g, w_ukv, conv_w,
           conv_b, conv_ln_g, conv_ln_b, w_pw2, sgu_ln_g, sgu_ln_b, w_s, b_s, w_out):
    B, S, D = x.shape
    L = w_ada.shape[0]
    tm = min(S, LAYER_IN_TILE)

    c_pad = jnp.concatenate([c, jnp.zeros((8 - B % 8, D), c.dtype)], axis=0) if B % 8 else c
    mod = _modulation(c_pad, w_ada, b_ada)
    mod = mod.reshape(L, mod.shape[1], 3, 1, D)
    tcos, tsin = _rope_tables(positions)

    vec = lambda a: a.reshape(L, 1, -1)
    conv_w_p = jnp.concatenate([conv_w, jnp.zeros((L, 1, conv_w.shape[-1]), F32)], axis=1)
    bs_wide = jnp.repeat(jnp.swapaxes(b_s, 1, 2), w_pw2.shape[-1] // SGU_GROUPS, axis=2)
    params = (vec(g_pre), _prep_w_in(w_in), vec(q_norm_g), _prep_w_uq(w_uq), vec(kv_norm_g),
              _prep_w_ukv(w_ukv), conv_w_p, vec(conv_b), vec(conv_ln_g), vec(conv_ln_b),
              w_pw2.astype(BF16), vec(sgu_ln_g), vec(sgu_ln_b), w_s, bs_wide)
    w_out_b = w_out.astype(BF16)
    g_post = vec(g_post)

    for l in range(L):
        qp, kp, v, gated = _layer_in(l, x, mod, tcos, tsin, params, tm)
        y_att = _attention(qp, kp, v)
        x = _layer_out(l, x, y_att, gated, mod, w_out_b, g_post, min(S, LAYER_OUT_TILE))
    return x
```

```python
import functools

import numpy as np
import jax
import jax.numpy as jnp
from jax import lax
from jax.experimental import pallas as pl
from jax.experimental.pallas import tpu as pltpu

F32 = jnp.float32
BF16 = jnp.bfloat16

NORM_EPS = 1e-6
ATT_HEADS = 4
QK_NOPE_DIM = 128
QK_ROPE_DIM = 64
V_HEAD_DIM = 128
HEAD_PAD = 256
V_PAD = 256
ROPE_THETA = 10000.0
CONV_KERNEL = 31
CONV_HALO = 32
CONV_ROW_BLOCK = 64
SGU_GROUPS = 4
SGU_CHUNK = 128
SQRT_HALF = float(np.sqrt(0.5).astype(np.float32))
ATT_SCALE = float((QK_NOPE_DIM + QK_ROPE_DIM) ** -0.5)
LOG2E = float(np.log2(np.e))
Q_SCALE = ATT_SCALE * LOG2E
VMEM_LIMIT = 56 * 1024 * 1024
LAYER_IN_TILE = 1024
LAYER_OUT_TILE = 1024


def _rms(x, g):
    return x * lax.rsqrt(jnp.mean(x * x, axis=-1, keepdims=True) + NORM_EPS) * g


def _layer_norm(x, g, b):
    mu = jnp.mean(x, axis=-1, keepdims=True)
    xc = x - mu
    var = jnp.mean(xc * xc, axis=-1, keepdims=True)
    return xc * lax.rsqrt(var + NORM_EPS) * g + b


def _sigmoid(x):
    return 1.0 / (1.0 + jnp.exp(-x))


def _silu(x):
    return x * _sigmoid(x)


def _gelu(x):
    return 0.5 * x * (1.0 + lax.erf(x * SQRT_HALF))


def _rope(t, tc, ts):
    lane = lax.broadcasted_iota(jnp.int32, t.shape, 1)
    first = lane < (QK_ROPE_DIM // 2)
    sw = jnp.where(first, pltpu.roll(t, 128 - QK_ROPE_DIM // 2, 1), pltpu.roll(t, QK_ROPE_DIM // 2, 1))
    return t * tc + jnp.where(first, -sw, sw) * ts


def _mod_kernel(c_ref, w_ref, b_ref, o_ref):
    c = c_ref[...]
    o_ref[...] = jnp.dot(_silu(c), w_ref[...], preferred_element_type=F32) + b_ref[...]


def _modulation(c_pad, w_ada, b_ada):
    L, D, D3 = w_ada.shape
    tn = 1024
    return pl.pallas_call(
        _mod_kernel,
        grid=(L, D3 // tn),
        in_specs=[
            pl.BlockSpec(c_pad.shape, lambda l, j: (0, 0)),
            pl.BlockSpec((None, D, tn), lambda l, j: (l, 0, j)),
            pl.BlockSpec((None, 1, tn), lambda l, j: (l, 0, j)),
        ],
        out_specs=pl.BlockSpec((None, c_pad.shape[0], tn), lambda l, j: (l, 0, j)),
        out_shape=jax.ShapeDtypeStruct((L, c_pad.shape[0], D3), F32),
        compiler_params=pltpu.CompilerParams(
            dimension_semantics=("arbitrary", "arbitrary"), vmem_limit_bytes=VMEM_LIMIT),
        name="adaln_mod",
    )(c_pad, w_ada, b_ada.reshape(L, 1, D3))


def _rope_table_kernel(pos_ref, invf_ref, cos_ref, sin_ref):
    half = QK_ROPE_DIM // 2
    per_row = 128 // half
    tr = pos_ref.shape[0]
    ang = pos_ref[...].astype(F32) * invf_ref[...]
    lane = lax.broadcasted_iota(jnp.int32, ang.shape, 1)
    for tab, out_ref in ((jnp.cos(ang), cos_ref), (jnp.sin(ang), sin_ref)):
        for k in range(per_row):
            lo = tab if k == 0 else pltpu.roll(tab, 128 - half * k, 1)
            hi = pltpu.roll(lo, half, 1)
            wide = jnp.where(lane < half, lo, jnp.where(lane < 2 * half, hi, 0.0))
            out_ref[pl.ds(k, tr, stride=per_row), :] = wide


def _rope_tables(positions):
    B, S = positions.shape
    half = QK_ROPE_DIM // 2
    per_row = 128 // half
    n_rows = B * S // per_row
    inv_freq = ROPE_THETA ** (-jnp.arange(0, QK_ROPE_DIM, 2, dtype=F32) / QK_ROPE_DIM)
    pos_wide = jnp.broadcast_to(
        positions.reshape(n_rows, per_row, 1), (n_rows, per_row, half)).reshape(n_rows, 128)
    invf_wide = jnp.tile(inv_freq, per_row).reshape(1, 128)
    tr = min(n_rows, 512)
    cos_w, sin_w = pl.pallas_call(
        _rope_table_kernel,
        grid=(n_rows // tr,),
        in_specs=[pl.BlockSpec((tr, 128), lambda i: (i, 0)), pl.BlockSpec((1, 128), lambda i: (0, 0))],
        out_specs=[pl.BlockSpec((tr * per_row, 128), lambda i: (i, 0))] * 2,
        out_shape=[jax.ShapeDtypeStruct((B * S, 128), F32)] * 2,
        compiler_params=pltpu.CompilerParams(dimension_semantics=("arbitrary",)),
        name="rope_table",
    )(pos_wide, invf_wide)
    return cos_w.reshape(B, S, 128), sin_w.reshape(B, S, 128)


_H_QLAT, _H_KVLAT, _H_KR, _H_END = 0, 256, 384, 512
_T_GATT, _T_CA, _T_CB, _T_GC, _T_SU, _T_SV, _T_GS, _T_END = 0, 512, 768, 1024, 1280, 1536, 1792, 2048


def _layer_in_kernel(x_ref, shift_ref, scale_ref, gpre_ref, winh_ref, wint_ref, qg_ref, wuq_ref, kvg_ref, wukv_ref,
                     tc_ref, ts_ref, cw_ref, cb_ref, clg_ref, clb_ref, wpw_ref,
                     slg_ref, slb_ref, ws_ref, bs_ref,
                     q_out, k_out, v_out, g_out, hbuf, shifted):
    tm = x_ref.shape[0]

    @pl.when(pl.program_id(1) == 0)
    def _():
        hbuf[0:CONV_HALO, :] = jnp.zeros((CONV_HALO, hbuf.shape[1]), F32)

    x = x_ref[...]
    h = _rms(x, gpre_ref[...]) * (1.0 + scale_ref[...]) + shift_ref[...]
    hb = h.astype(BF16)

    def proj(w_ref, a, b):
        return jnp.dot(hb, w_ref[:, a:b], preferred_element_type=F32)

    tc = tc_ref[...]
    ts = ts_ref[...]

    glu = proj(wint_ref, _T_CA, _T_CB) * _sigmoid(proj(wint_ref, _T_CB, _T_GC))
    hbuf[CONV_HALO:CONV_HALO + tm, :] = glu
    base = CONV_HALO - (CONV_KERNEL - 1)
    z_q = proj(winh_ref, _H_QLAT, _H_KVLAT)
    z_kv = proj(winh_ref, _H_KVLAT, _H_KR)
    z_kr = proj(winh_ref, _H_KR, _H_END)
    for b in range(1, 8):
        shifted[b - 1, :, :] = hbuf[pl.ds(b, shifted.shape[1]), :]

    def conv_taps(r0):
        acc = jnp.broadcast_to(cb_ref[...], (CONV_ROW_BLOCK, glu.shape[1]))
        for k in range(CONV_KERNEL):
            a, b = divmod(base + k, 8)
            rows = pl.ds(r0 + 8 * a, CONV_ROW_BLOCK)
            src = hbuf[rows, :] if b == 0 else shifted[b - 1, rows, :]
            acc = acc + cw_ref[k:k + 1, :] * src
        return acc

    n_blk = tm // CONV_ROW_BLOCK
    taps = lambda part: [conv_taps(r * CONV_ROW_BLOCK)
                         for r in range(part * n_blk // 4, (part + 1) * n_blk // 4)]
    qn = _rms(z_q, qg_ref[...]).astype(BF16)
    kvn = _rms(z_kv, kvg_ref[...]).astype(BF16)
    q = jnp.dot(qn, wuq_ref[...], preferred_element_type=F32)
    kv = jnp.dot(kvn, wukv_ref[...], preferred_element_type=F32)
    blocks = taps(0) + taps(1)
    kr = _rope(z_kr, tc, ts).astype(BF16)
    for hd in range(ATT_HEADS):
        o = hd * HEAD_PAD
        q_out[:, o:o + QK_NOPE_DIM] = (q[:, o:o + QK_NOPE_DIM] * Q_SCALE).astype(BF16)
        q_out[:, o + QK_NOPE_DIM:o + HEAD_PAD] = (
            _rope(q[:, o + QK_NOPE_DIM:o + HEAD_PAD], tc, ts) * Q_SCALE).astype(BF16)
        k_out[:, o:o + QK_NOPE_DIM] = kv[:, hd * QK_NOPE_DIM:(hd + 1) * QK_NOPE_DIM].astype(BF16)
        k_out[:, o + QK_NOPE_DIM:o + HEAD_PAD] = kr
        v_out[:, hd * V_PAD:hd * V_PAD + V_HEAD_DIM] = kv[
            :, (ATT_HEADS + hd) * QK_NOPE_DIM:(ATT_HEADS + hd + 1) * QK_NOPE_DIM].astype(BF16)
        v_out[:, hd * V_PAD + V_HEAD_DIM:(hd + 1) * V_PAD] = jnp.ones((tm, V_PAD - V_HEAD_DIM), BF16)

    g_out[:, 0:512] = _silu(proj(wint_ref, _T_GATT, _T_CA)).astype(BF16)
    blocks += taps(2)

    vn = _layer_norm(_gelu(proj(wint_ref, _T_SV, _T_GS)), slg_ref[...], slb_ref[...]).astype(BF16)
    s_gc = _silu(proj(wint_ref, _T_GC, _T_SU))
    blocks += taps(3)
    hbuf[0:CONV_HALO, :] = hbuf[tm:tm + CONV_HALO, :]
    z_su = proj(wint_ref, _T_SU, _T_SV)
    z_gs = proj(wint_ref, _T_GS, _T_END)

    hc = _silu(_layer_norm(jnp.concatenate(blocks, axis=0), clg_ref[...], clb_ref[...])).astype(BF16)
    yc = jnp.dot(hc, wpw_ref[...], preferred_element_type=F32) * s_gc
    g_out[:, 512:768] = yc.astype(BF16)

    r = lax.broadcasted_iota(jnp.int32, (SGU_CHUNK, SGU_CHUNK), 0)
    c = lax.broadcasted_iota(jnp.int32, (SGU_CHUNK, SGU_CHUNK), 1)
    w_stack = jnp.concatenate(
        [jnp.where(c <= r, ws_ref[g], 0.0) for g in range(SGU_GROUPS)], axis=0).astype(BF16)
    gw = vn.shape[1] // SGU_GROUPS
    lane_grp = lax.broadcasted_iota(jnp.int32, (SGU_CHUNK, vn.shape[1]), 1) // gw
    svs = []
    for ck in range(tm // SGU_CHUNK):
        full = jnp.dot(w_stack, vn[ck * SGU_CHUNK:(ck + 1) * SGU_CHUNK, :], preferred_element_type=F32)
        sv = full[0:SGU_CHUNK]
        for g in range(1, SGU_GROUPS):
            sv = jnp.where(lane_grp == g, full[g * SGU_CHUNK:(g + 1) * SGU_CHUNK], sv)
        svs.append(sv + bs_ref[...])
    sv = jnp.concatenate(svs, axis=0)
    ys = _gelu(z_su) * sv * _silu(z_gs)
    g_out[:, 768:1024] = ys.astype(BF16)


def _mod_spec(l, part, D):
    return pl.BlockSpec((None, None, None, 1, D), lambda b, i: (l, b, part, 0, 0))


def _layer_spec(l, a):
    return pl.BlockSpec((None,) + a.shape[1:], lambda b, i: (l,) + (0,) * (a.ndim - 1))


def _layer_in(l, x, mod, tcos, tsin, params, tm):
    B, S, D = x.shape
    cw = params[8].shape[-1]
    seq = lambda w: pl.BlockSpec((None, tm, w), lambda b, i: (b, i, 0))
    args = [x, mod, mod] + list(params[:7]) + [tcos, tsin] + list(params[7:])
    in_specs = ([seq(D), _mod_spec(l, 0, D), _mod_spec(l, 1, D)] + [_layer_spec(l, a) for a in params[:7]]
                + [seq(128), seq(128)] + [_layer_spec(l, a) for a in params[7:]])
    hp = ATT_HEADS * HEAD_PAD
    return pl.pallas_call(
        _layer_in_kernel,
        grid=(B, S // tm),
        in_specs=in_specs,
        out_specs=[seq(hp), seq(hp), seq(ATT_HEADS * V_PAD), seq(D)],
        out_shape=[jax.ShapeDtypeStruct((B, S, hp), BF16), jax.ShapeDtypeStruct((B, S, hp), BF16),
                   jax.ShapeDtypeStruct((B, S, ATT_HEADS * V_PAD), BF16),
                   jax.ShapeDtypeStruct((B, S, D), BF16)],
        scratch_shapes=[pltpu.VMEM((CONV_HALO + tm, cw), F32),
                        pltpu.VMEM((7, CONV_HALO + tm - 8, cw), F32)],
        compiler_params=pltpu.CompilerParams(
            dimension_semantics=("arbitrary", "arbitrary"), vmem_limit_bytes=VMEM_LIMIT),
        name="layer_in",
    )(*args)


ATT_TILE = 512
ATT_ROW_BLOCK = 16


def _attn_kernel(q_ref, k_ref, v_ref, g_ref, o_ref, s_a, s_b, s_c, p_a, p_b, p_c, al_a, al_b, al_c, m_st, acc_st):
    t = s_a.shape[0]
    nq = q_ref.shape[0] // t
    lanes = m_st.shape[2]
    n_col = t // lanes
    s_ring = (s_a, s_b, s_c)
    p_ring = ((p_a, al_a), (p_b, al_b), (p_c, al_c))

    def tile_rows(idx):
        if isinstance(idx, int):
            return pl.ds(idx * t, t)
        return pl.ds(pl.multiple_of(idx * t, t), t)

    def score_tile(s_ref, i, j):
        s_ref[...] = lax.dot_general(q_ref[tile_rows(i), :], k_ref[tile_rows(j), :],
                                     (((1,), (1,)), ((), ())), preferred_element_type=F32)

    def softmax_diag(s_ref, p_buf, i):
        p_ref, _ = p_buf
        for r0 in range(0, t, ATT_ROW_BLOCK):
            rows = pl.ds(r0, ATT_ROW_BLOCK)
            live = [c for c in range(n_col) if c * lanes <= r0 + ATT_ROW_BLOCK - 1]
            s = {}
            for c in live:
                sc = s_ref[rows, c * lanes:(c + 1) * lanes]
                if (c + 1) * lanes - 1 > r0:
                    row = r0 + lax.broadcasted_iota(jnp.int32, (ATT_ROW_BLOCK, lanes), 0)
                    col = c * lanes + lax.broadcasted_iota(jnp.int32, (ATT_ROW_BLOCK, lanes), 1)
                    sc = jnp.where(col <= row, sc, -jnp.inf)
                s[c] = sc
            m_new = jnp.max(functools.reduce(jnp.maximum, s.values()), axis=-1, keepdims=True)
            p = {c: jnp.exp2(s[c] - m_new) for c in live}
            m_st[i, rows, :] = jnp.broadcast_to(m_new, (ATT_ROW_BLOCK, lanes))
            for c in range(n_col):
                p_ref[rows, c * lanes:(c + 1) * lanes] = (
                    p[c].astype(BF16) if c in p else jnp.zeros((ATT_ROW_BLOCK, lanes), BF16))

    def softmax_lower(s_ref, p_buf, i):
        p_ref, al_ref = p_buf
        for r0 in range(0, t, ATT_ROW_BLOCK):
            rows = pl.ds(r0, ATT_ROW_BLOCK)
            s = [s_ref[rows, c * lanes:(c + 1) * lanes] for c in range(n_col)]
            m_old = m_st[i, rows, :]
            m_new = jnp.maximum(m_old, jnp.max(functools.reduce(jnp.maximum, s), axis=-1, keepdims=True))
            p = [jnp.exp2(sc - m_new) for sc in s]
            alpha = jnp.exp2(m_old - m_new)
            m_st[i, rows, :] = m_new
            al_ref[rows, :] = alpha
            for c in range(n_col):
                p_ref[rows, c * lanes:(c + 1) * lanes] = p[c].astype(BF16)

    def value_first(p_buf, i, j):
        acc_st[i] = jnp.dot(p_buf[0][...], v_ref[tile_rows(j), :], preferred_element_type=F32)

    def value_accumulate(p_buf, i, j):
        al = p_buf[1][...]
        acc_st[i] = jnp.concatenate([al, al], axis=1) * acc_st[i] + jnp.dot(
            p_buf[0][...], v_ref[tile_rows(j), :], preferred_element_type=F32)

    if nq == 1:
        score_tile(s_ring[0], 0, 0)
        softmax_diag(s_ring[0], p_ring[0], 0)
        value_first(p_ring[0], 0, 0)
    else:
        u1 = 4 if nq % 4 == 0 else 2
        assert nq % u1 == 0
        p_b[...] = jnp.zeros(p_b.shape, BF16)
        score_tile(s_ring[0], 0, 0)

        def diag_steps(it, carry):
            for d in range(u1):
                i = it * u1 + d
                cur, other = d % 2, (d + 1) % 2
                nxt = jnp.minimum(i + 1, nq - 1)
                prv = jnp.maximum(i - 1, 0)
                score_tile(s_ring[other], nxt, nxt)
                softmax_diag(s_ring[cur], p_ring[cur], i)
                value_first(p_ring[other], prv, prv)
            return carry

        lax.fori_loop(0, nq // u1, diag_steps, 0)
        value_first(p_ring[(nq - 1) % 2], nq - 1, nq - 1)

    order = [(i, j) for i in range(1, nq) for j in range(i)]
    n_steps = len(order)

    def lower_step(n, cur_ij, prev_ij, next_ij):
        score_tile(s_ring[(n + 1) % 3], *next_ij)
        softmax_lower(s_ring[n % 3], p_ring[n % 3], cur_ij[0])
        value_accumulate(p_ring[(n - 1) % 3], *prev_ij)

    if n_steps > 0:
        p_c[...] = jnp.zeros(p_c.shape, BF16)
        al_c[...] = jnp.ones(al_c.shape, F32)
        score_tile(s_ring[0], 1, 0)

    unrolls = [u for u in (24, 12, 6, 3) if n_steps % u == 0]
    if not unrolls:
        assert n_steps <= 32
        for n, ij in enumerate(order):
            lower_step(n, ij, order[max(n - 1, 0)], order[min(n + 1, n_steps - 1)])
        value_accumulate(p_ring[(n_steps - 1) % 3], *order[-1])
    elif n_steps > 0:
        u2 = unrolls[0]

        def advance(i, j):
            wrap = (j + 1) == i
            return jnp.where(wrap, i + 1, i), jnp.where(wrap, 0, j + 1)

        def softmax_lower_values(s, i):
            p_rows, al_rows = [], []
            for r0 in range(0, t, ATT_ROW_BLOCK):
                rows = pl.ds(r0, ATT_ROW_BLOCK)
                sb = [s[r0:r0 + ATT_ROW_BLOCK, c * lanes:(c + 1) * lanes] for c in range(n_col)]
                m_old = m_st[i, rows, :]
                m_new = jnp.maximum(
                    m_old, jnp.max(functools.reduce(jnp.maximum, sb), axis=-1, keepdims=True))
                p = [jnp.exp2(sc - m_new) for sc in sb]
                alpha = jnp.exp2(m_old - m_new)
                m_st[i, rows, :] = m_new
                p_rows.append(jnp.concatenate([pc.astype(BF16) for pc in p], axis=1))
                al_rows.append(alpha)
            return jnp.concatenate(p_rows, axis=0), jnp.concatenate(al_rows, axis=0)

        def lower_steps(it, carry):
            i, j, ip, jp = carry
            s_cur, p_prev, al_prev = s_a[...], p_c[...], al_c[...]
            for _ in range(u2):
                i_n, j_n = advance(i, j)
                s_nxt = lax.dot_general(q_ref[tile_rows(jnp.minimum(i_n, nq - 1)), :], k_ref[tile_rows(j_n), :],
                                        (((1,), (1,)), ((), ())), preferred_element_type=F32)
                p_cur, al_cur = softmax_lower_values(s_cur, i)
                acc_st[ip] = jnp.concatenate([al_prev, al_prev], axis=1) * acc_st[ip] + jnp.dot(
                    p_prev, v_ref[tile_rows(jp), :], preferred_element_type=F32)
                s_cur, p_prev, al_prev = s_nxt, p_cur, al_cur
                i, j, ip, jp = i_n, j_n, i, j
            s_a[...] = s_cur
            p_c[...] = p_prev
            al_c[...] = al_prev
            return i, j, ip, jp

        one = jnp.int32(1)
        zero = jnp.int32(0)
        _, _, ip, jp = lax.fori_loop(0, n_steps // u2, lower_steps, (one, zero, one, zero))
        value_accumulate(p_ring[2], ip, jp)

    def finish(i, carry):
        acc = acc_st[i]
        gate = g_ref[tile_rows(i), :].astype(F32)
        o_ref[tile_rows(i), :] = (acc[:, :lanes] / acc[:, lanes:] * gate).astype(o_ref.dtype)
        return carry

    lax.fori_loop(0, nq, finish, 0)


def _attention(qp, kp, v, gated):
    B, S, _ = qp.shape
    t = min(S, ATT_TILE)
    nq = S // t
    lanes = V_HEAD_DIM
    head_block = lambda w: pl.BlockSpec((None, S, w), lambda b, h: (b, 0, h))
    return pl.pallas_call(
        _attn_kernel,
        grid=(B, ATT_HEADS),
        in_specs=[head_block(HEAD_PAD), head_block(HEAD_PAD), head_block(V_PAD), head_block(V_HEAD_DIM)],
        out_specs=head_block(V_HEAD_DIM),
        out_shape=jax.ShapeDtypeStruct((B, S, ATT_HEADS * V_HEAD_DIM), BF16),
        scratch_shapes=[pltpu.VMEM((t, t), F32)] * 3
                       + [pltpu.VMEM((t, t), BF16)] * 3
                       + [pltpu.VMEM((t, lanes), F32)] * 3
                       + [
                        pltpu.VMEM((nq, t, lanes), F32),
                        pltpu.VMEM((nq, t, V_PAD), F32)],
        compiler_params=pltpu.CompilerParams(
            dimension_semantics=("arbitrary", "arbitrary"), vmem_limit_bytes=VMEM_LIMIT),
        name="mla_attention",
    )(qp, kp, v, gated)


def _layer_out_kernel(x_ref, y_ref, g_ref, gate_ref, wout_ref, gpost_ref, o_ref):
    aw = y_ref.shape[1]
    y = (jnp.dot(y_ref[...], wout_ref[0:aw, :], preferred_element_type=F32)
         + jnp.dot(g_ref[...], wout_ref[aw:, :], preferred_element_type=F32))
    o_ref[...] = x_ref[...] + gate_ref[...] * _rms(y, gpost_ref[...])


def _layer_out(l, x, y_att, gated, mod, w_out_b, g_post, tm):
    B, S, D = x.shape
    seq = lambda w: pl.BlockSpec((None, tm, w), lambda b, i: (b, i, 0))
    return pl.pallas_call(
        _layer_out_kernel,
        grid=(B, S // tm),
        in_specs=[seq(D), seq(y_att.shape[-1]),
                  pl.BlockSpec((None, tm, gated.shape[-1] - y_att.shape[-1]), lambda b, i: (b, i, 1)),
                  _mod_spec(l, 2, D),
                  _layer_spec(l, w_out_b), _layer_spec(l, g_post)],
        out_specs=seq(D),
        out_shape=jax.ShapeDtypeStruct((B, S, D), F32),
        compiler_params=pltpu.CompilerParams(
            dimension_semantics=("arbitrary", "arbitrary"), vmem_limit_bytes=VMEM_LIMIT),
        name="layer_out",
    )(x, y_att, gated, mod, w_out_b, g_post)


def _prep_w_in(w):
    cut = _H_KR + QK_ROPE_DIM
    pad = jnp.zeros(w.shape[:-1] + (_H_END - cut,), w.dtype)
    return jnp.concatenate([w[..., :cut], pad], axis=-1).astype(BF16), w[..., cut:].astype(BF16)


def _prep_w_uq(w):
    lead = w.shape[:-1]
    w = w.reshape(lead + (ATT_HEADS, QK_NOPE_DIM + QK_ROPE_DIM))
    pad = jnp.zeros(lead + (ATT_HEADS, HEAD_PAD - QK_NOPE_DIM - QK_ROPE_DIM), w.dtype)
    return jnp.concatenate([w, pad], axis=-1).reshape(lead + (ATT_HEADS * HEAD_PAD,)).astype(BF16)


def _prep_w_ukv(w):
    lead = w.shape[:-1]
    w = w.reshape(lead + (ATT_HEADS, QK_NOPE_DIM + V_HEAD_DIM))
    return jnp.concatenate([w[..., :QK_NOPE_DIM].reshape(lead + (-1,)),
                            w[..., QK_NOPE_DIM:].reshape(lead + (-1,))], axis=-1).astype(BF16)


def kernel(x, c, positions, w_ada, b_ada, g_pre, g_post, w_in, q_norm_g, w_uq, kv_norm_g, w_ukv, conv_w,
           conv_b, conv_ln_g, conv_ln_b, w_pw2, sgu_ln_g, sgu_ln_b, w_s, b_s, w_out):
    B, S, D = x.shape
    L = w_ada.shape[0]
    tm = min(S, LAYER_IN_TILE)

    c_pad = jnp.concatenate([c, jnp.zeros((8 - B % 8, D), c.dtype)], axis=0) if B % 8 else c
    mod = _modulation(c_pad, w_ada, b_ada)
    mod = mod.reshape(L, mod.shape[1], 3, 1, D)
    tcos, tsin = _rope_tables(positions)

    vec = lambda a: a.reshape(L, 1, -1)
    conv_w_p = jnp.concatenate([conv_w, jnp.zeros((L, 1, conv_w.shape[-1]), F32)], axis=1)
    bs_wide = jnp.repeat(jnp.swapaxes(b_s, 1, 2), w_pw2.shape[-1] // SGU_GROUPS, axis=2)
    params = (vec(g_pre), *_prep_w_in(w_in), vec(q_norm_g), _prep_w_uq(w_uq), vec(kv_norm_g),
              _prep_w_ukv(w_ukv), conv_w_p, vec(conv_b), vec(conv_ln_g), vec(conv_ln_b),
              w_pw2.astype(BF16), vec(sgu_ln_g), vec(sgu_ln_b), w_s, bs_wide)
    w_out_b = w_out.astype(BF16)
    g_post = vec(g_post)

    for l in range(L):
        qp, kp, v, gated = _layer_in(l, x, mod, tcos, tsin, params, tm)
        y_att = _attention(qp, kp, v, gated)
        x = _layer_out(l, x, y_att, gated, mod, w_out_b, g_post, min(S, LAYER_OUT_TILE))
    return x
```

```python
import functools

import numpy as np
import jax
import jax.numpy as jnp
from jax import lax
from jax.experimental import pallas as pl
from jax.experimental.pallas import tpu as pltpu

F32 = jnp.float32
BF16 = jnp.bfloat16

NORM_EPS = 1e-6
ATT_HEADS = 4
QK_NOPE_DIM = 128
QK_ROPE_DIM = 64
V_HEAD_DIM = 128
HEAD_PAD = 256
V_PAD = 256
ROPE_THETA = 10000.0
CONV_KERNEL = 31
CONV_HALO = 32
CONV_ROW_BLOCK = 64
SGU_GROUPS = 4
SGU_CHUNK = 128
SQRT_HALF = float(np.sqrt(0.5).astype(np.float32))
ATT_SCALE = float((QK_NOPE_DIM + QK_ROPE_DIM) ** -0.5)
LOG2E = float(np.log2(np.e))
Q_SCALE = ATT_SCALE * LOG2E
VMEM_LIMIT = 56 * 1024 * 1024
LAYER_IN_TILE = 1024
LAYER_OUT_TILE = 1024


def _rms(x, g):
    return x * lax.rsqrt(jnp.mean(x * x, axis=-1, keepdims=True) + NORM_EPS) * g


def _layer_norm(x, g, b):
    mu = jnp.mean(x, axis=-1, keepdims=True)
    xc = x - mu
    var = jnp.mean(xc * xc, axis=-1, keepdims=True)
    return xc * lax.rsqrt(var + NORM_EPS) * g + b


def _sigmoid(x):
    return 1.0 / (1.0 + jnp.exp(-x))


def _silu(x):
    return x * _sigmoid(x)


def _gelu(x):
    return 0.5 * x * (1.0 + lax.erf(x * SQRT_HALF))


def _rope(t, tc, ts):
    lane = lax.broadcasted_iota(jnp.int32, t.shape, 1)
    first = lane < (QK_ROPE_DIM // 2)
    sw = jnp.where(first, pltpu.roll(t, 128 - QK_ROPE_DIM // 2, 1), pltpu.roll(t, QK_ROPE_DIM // 2, 1))
    return t * tc + jnp.where(first, -sw, sw) * ts


def _mod_kernel(c_ref, w_ref, b_ref, o_ref):
    c = c_ref[...]
    o_ref[...] = jnp.dot(_silu(c), w_ref[...], preferred_element_type=F32) + b_ref[...]


def _modulation(c_pad, w_ada, b_ada):
    L, D, D3 = w_ada.shape
    tn = 1024
    return pl.pallas_call(
        _mod_kernel,
        grid=(L, D3 // tn),
        in_specs=[
            pl.BlockSpec(c_pad.shape, lambda l, j: (0, 0)),
            pl.BlockSpec((None, D, tn), lambda l, j: (l, 0, j)),
            pl.BlockSpec((None, 1, tn), lambda l, j: (l, 0, j)),
        ],
        out_specs=pl.BlockSpec((None, c_pad.shape[0], tn), lambda l, j: (l, 0, j)),
        out_shape=jax.ShapeDtypeStruct((L, c_pad.shape[0], D3), F32),
        compiler_params=pltpu.CompilerParams(
            dimension_semantics=("arbitrary", "arbitrary"), vmem_limit_bytes=VMEM_LIMIT),
        name="adaln_mod",
    )(c_pad, w_ada, b_ada.reshape(L, 1, D3))


def _rope_table_kernel(pos_ref, invf_ref, cos_ref, sin_ref):
    half = QK_ROPE_DIM // 2
    per_row = 128 // half
    tr = pos_ref.shape[0]
    ang = pos_ref[...].astype(F32) * invf_ref[...]
    lane = lax.broadcasted_iota(jnp.int32, ang.shape, 1)
    for tab, out_ref in ((jnp.cos(ang), cos_ref), (jnp.sin(ang), sin_ref)):
        for k in range(per_row):
            lo = tab if k == 0 else pltpu.roll(tab, 128 - half * k, 1)
            hi = pltpu.roll(lo, half, 1)
            wide = jnp.where(lane < half, lo, jnp.where(lane < 2 * half, hi, 0.0))
            out_ref[pl.ds(k, tr, stride=per_row), :] = wide


def _rope_tables(positions):
    B, S = positions.shape
    half = QK_ROPE_DIM // 2
    per_row = 128 // half
    n_rows = B * S // per_row
    inv_freq = ROPE_THETA ** (-jnp.arange(0, QK_ROPE_DIM, 2, dtype=F32) / QK_ROPE_DIM)
    pos_wide = jnp.broadcast_to(
        positions.reshape(n_rows, per_row, 1), (n_rows, per_row, half)).reshape(n_rows, 128)
    invf_wide = jnp.tile(inv_freq, per_row).reshape(1, 128)
    tr = min(n_rows, 512)
    cos_w, sin_w = pl.pallas_call(
        _rope_table_kernel,
        grid=(n_rows // tr,),
        in_specs=[pl.BlockSpec((tr, 128), lambda i: (i, 0)), pl.BlockSpec((1, 128), lambda i: (0, 0))],
        out_specs=[pl.BlockSpec((tr * per_row, 128), lambda i: (i, 0))] * 2,
        out_shape=[jax.ShapeDtypeStruct((B * S, 128), F32)] * 2,
        compiler_params=pltpu.CompilerParams(dimension_semantics=("arbitrary",)),
        name="rope_table",
    )(pos_wide, invf_wide)
    return cos_w.reshape(B, S, 128), sin_w.reshape(B, S, 128)


_H_QLAT, _H_KVLAT, _H_KR, _H_END = 0, 256, 384, 512
_T_GATT, _T_CA, _T_CB, _T_GC, _T_SU, _T_SV, _T_GS, _T_END = 0, 512, 768, 1024, 1280, 1536, 1792, 2048


def _layer_in_kernel(x_ref, shift_ref, scale_ref, gpre_ref, winh_ref, wint_ref, qg_ref, wuq_ref, kvg_ref, wukv_ref,
                     tc_ref, ts_ref, cw_ref, cb_ref, clg_ref, clb_ref, wpw_ref,
                     slg_ref, slb_ref, ws_ref, bs_ref,
                     q_out, k_out, v_out, g_out, hbuf, shifted):
    tm = x_ref.shape[0]

    @pl.when(pl.program_id(1) == 0)
    def _():
        hbuf[0:CONV_HALO, :] = jnp.zeros((CONV_HALO, hbuf.shape[1]), F32)

    x = x_ref[...]
    h = _rms(x, gpre_ref[...]) * (1.0 + scale_ref[...]) + shift_ref[...]
    hb = h.astype(BF16)

    def proj(w_ref, a, b):
        return jnp.dot(hb, w_ref[:, a:b], preferred_element_type=F32)

    tc = tc_ref[...]
    ts = ts_ref[...]

    glu = proj(wint_ref, _T_CA, _T_CB) * _sigmoid(proj(wint_ref, _T_CB, _T_GC))
    hbuf[CONV_HALO:CONV_HALO + tm, :] = glu
    base = CONV_HALO - (CONV_KERNEL - 1)
    z_q = proj(winh_ref, _H_QLAT, _H_KVLAT)
    z_kv = proj(winh_ref, _H_KVLAT, _H_KR)
    z_kr = proj(winh_ref, _H_KR, _H_END)
    for b in range(1, 8):
        shifted[b - 1, :, :] = hbuf[pl.ds(b, shifted.shape[1]), :]

    def conv_taps(r0):
        acc = jnp.broadcast_to(cb_ref[...], (CONV_ROW_BLOCK, glu.shape[1]))
        for k in range(CONV_KERNEL):
            a, b = divmod(base + k, 8)
            rows = pl.ds(r0 + 8 * a, CONV_ROW_BLOCK)
            src = hbuf[rows, :] if b == 0 else shifted[b - 1, rows, :]
            acc = acc + cw_ref[k:k + 1, :] * src
        return acc

    n_blk = tm // CONV_ROW_BLOCK
    taps = lambda part: [conv_taps(r * CONV_ROW_BLOCK)
                         for r in range(part * n_blk // 4, (part + 1) * n_blk // 4)]
    qn = _rms(z_q, qg_ref[...]).astype(BF16)
    kvn = _rms(z_kv, kvg_ref[...]).astype(BF16)
    q = jnp.dot(qn, wuq_ref[...], preferred_element_type=F32)
    kv = jnp.dot(kvn, wukv_ref[...], preferred_element_type=F32)
    blocks = taps(0) + taps(1)
    kr = _rope(z_kr, tc, ts).astype(BF16)
    for hd in range(ATT_HEADS):
        o = hd * HEAD_PAD
        q_out[:, o:o + QK_NOPE_DIM] = (q[:, o:o + QK_NOPE_DIM] * Q_SCALE).astype(BF16)
        q_out[:, o + QK_NOPE_DIM:o + HEAD_PAD] = (
            _rope(q[:, o + QK_NOPE_DIM:o + HEAD_PAD], tc, ts) * Q_SCALE).astype(BF16)
        k_out[:, o:o + QK_NOPE_DIM] = kv[:, hd * QK_NOPE_DIM:(hd + 1) * QK_NOPE_DIM].astype(BF16)
        k_out[:, o + QK_NOPE_DIM:o + HEAD_PAD] = kr
        v_out[:, hd * V_PAD:hd * V_PAD + V_HEAD_DIM] = kv[
            :, (ATT_HEADS + hd) * QK_NOPE_DIM:(ATT_HEADS + hd + 1) * QK_NOPE_DIM].astype(BF16)
        v_out[:, hd * V_PAD + V_HEAD_DIM:(hd + 1) * V_PAD] = jnp.ones((tm, V_PAD - V_HEAD_DIM), BF16)

    g_out[:, 0:512] = _silu(proj(wint_ref, _T_GATT, _T_CA)).astype(BF16)
    blocks += taps(2)

    vn = _layer_norm(_gelu(proj(wint_ref, _T_SV, _T_GS)), slg_ref[...], slb_ref[...]).astype(BF16)
    s_gc = _silu(proj(wint_ref, _T_GC, _T_SU))
    blocks += taps(3)
    hbuf[0:CONV_HALO, :] = hbuf[tm:tm + CONV_HALO, :]
    z_su = proj(wint_ref, _T_SU, _T_SV)
    z_gs = proj(wint_ref, _T_GS, _T_END)

    hc = _silu(_layer_norm(jnp.concatenate(blocks, axis=0), clg_ref[...], clb_ref[...])).astype(BF16)
    yc = jnp.dot(hc, wpw_ref[...], preferred_element_type=F32) * s_gc
    g_out[:, 512:768] = yc.astype(BF16)

    r = lax.broadcasted_iota(jnp.int32, (SGU_CHUNK, SGU_CHUNK), 0)
    c = lax.broadcasted_iota(jnp.int32, (SGU_CHUNK, SGU_CHUNK), 1)
    w_stack = jnp.concatenate(
        [jnp.where(c <= r, ws_ref[g], 0.0) for g in range(SGU_GROUPS)], axis=0).astype(BF16)
    gw = vn.shape[1] // SGU_GROUPS
    lane_grp = lax.broadcasted_iota(jnp.int32, (SGU_CHUNK, vn.shape[1]), 1) // gw
    svs = []
    for ck in range(tm // SGU_CHUNK):
        full = jnp.dot(w_stack, vn[ck * SGU_CHUNK:(ck + 1) * SGU_CHUNK, :], preferred_element_type=F32)
        sv = full[0:SGU_CHUNK]
        for g in range(1, SGU_GROUPS):
            sv = jnp.where(lane_grp == g, full[g * SGU_CHUNK:(g + 1) * SGU_CHUNK], sv)
        svs.append(sv + bs_ref[...])
    sv = jnp.concatenate(svs, axis=0)
    ys = _gelu(z_su) * sv * _silu(z_gs)
    g_out[:, 768:1024] = ys.astype(BF16)


def _mod_spec(l, part, D):
    return pl.BlockSpec((None, None, None, 1, D), lambda b, i: (l, b, part, 0, 0))


def _layer_spec(l, a):
    return pl.BlockSpec((None,) + a.shape[1:], lambda b, i: (l,) + (0,) * (a.ndim - 1))


def _layer_in(l, x, mod, tcos, tsin, params, tm):
    B, S, D = x.shape
    cw = params[8].shape[-1]
    seq = lambda w: pl.BlockSpec((None, tm, w), lambda b, i: (b, i, 0))
    args = [x, mod, mod] + list(params[:7]) + [tcos, tsin] + list(params[7:])
    in_specs = ([seq(D), _mod_spec(l, 0, D), _mod_spec(l, 1, D)] + [_layer_spec(l, a) for a in params[:7]]
                + [seq(128), seq(128)] + [_layer_spec(l, a) for a in params[7:]])
    hp = ATT_HEADS * HEAD_PAD
    return pl.pallas_call(
        _layer_in_kernel,
        grid=(B, S // tm),
        in_specs=in_specs,
        out_specs=[seq(hp), seq(hp), seq(ATT_HEADS * V_PAD), seq(D)],
        out_shape=[jax.ShapeDtypeStruct((B, S, hp), BF16), jax.ShapeDtypeStruct((B, S, hp), BF16),
                   jax.ShapeDtypeStruct((B, S, ATT_HEADS * V_PAD), BF16),
                   jax.ShapeDtypeStruct((B, S, D), BF16)],
        scratch_shapes=[pltpu.VMEM((CONV_HALO + tm, cw), F32),
                        pltpu.VMEM((7, CONV_HALO + tm - 8, cw), F32)],
        compiler_params=pltpu.CompilerParams(
            dimension_semantics=("arbitrary", "arbitrary"), vmem_limit_bytes=VMEM_LIMIT),
        name="layer_in",
    )(*args)


ATT_TILE = 512
ATT_ROW_BLOCK = 16


def _attn_kernel(q_ref, k_ref, v_ref, g_ref, o_ref, s_a, s_b, s_c, p_a, p_b, p_c, al_a, al_b, al_c, m_st, acc_st):
    t = s_a.shape[0]
    nq = q_ref.shape[0] // t
    lanes = m_st.shape[2]
    n_col = t // lanes
    s_ring = (s_a, s_b, s_c)
    p_ring = ((p_a, al_a), (p_b, al_b), (p_c, al_c))

    def tile_rows(idx):
        if isinstance(idx, int):
            return pl.ds(idx * t, t)
        return pl.ds(pl.multiple_of(idx * t, t), t)

    def score_tile(s_ref, i, j):
        s_ref[...] = lax.dot_general(q_ref[tile_rows(i), :], k_ref[tile_rows(j), :],
                                     (((1,), (1,)), ((), ())), preferred_element_type=F32)

    def softmax_diag(s_ref, p_buf, i):
        p_ref, _ = p_buf
        for r0 in range(0, t, ATT_ROW_BLOCK):
            rows = pl.ds(r0, ATT_ROW_BLOCK)
            live = [c for c in range(n_col) if c * lanes <= r0 + ATT_ROW_BLOCK - 1]
            s = {}
            for c in live:
                sc = s_ref[rows, c * lanes:(c + 1) * lanes]
                if (c + 1) * lanes - 1 > r0:
                    row = r0 + lax.broadcasted_iota(jnp.int32, (ATT_ROW_BLOCK, lanes), 0)
                    col = c * lanes + lax.broadcasted_iota(jnp.int32, (ATT_ROW_BLOCK, lanes), 1)
                    sc = jnp.where(col <= row, sc, -jnp.inf)
                s[c] = sc
            m_new = jnp.max(functools.reduce(jnp.maximum, s.values()), axis=-1, keepdims=True)
            p = {c: jnp.exp2(s[c] - m_new) for c in live}
            m_st[i, rows, :] = jnp.broadcast_to(m_new, (ATT_ROW_BLOCK, lanes))
            for c in range(n_col):
                p_ref[rows, c * lanes:(c + 1) * lanes] = (
                    p[c].astype(BF16) if c in p else jnp.zeros((ATT_ROW_BLOCK, lanes), BF16))

    def softmax_lower(s_ref, p_buf, i):
        p_ref, al_ref = p_buf
        for r0 in range(0, t, ATT_ROW_BLOCK):
            rows = pl.ds(r0, ATT_ROW_BLOCK)
            s = [s_ref[rows, c * lanes:(c + 1) * lanes] for c in range(n_col)]
            m_old = m_st[i, rows, :]
            m_new = jnp.maximum(m_old, jnp.max(functools.reduce(jnp.maximum, s), axis=-1, keepdims=True))
            p = [jnp.exp2(sc - m_new) for sc in s]
            alpha = jnp.exp2(m_old - m_new)
            m_st[i, rows, :] = m_new
            al_ref[rows, :] = alpha
            for c in range(n_col):
                p_ref[rows, c * lanes:(c + 1) * lanes] = p[c].astype(BF16)

    def value_first(p_buf, i, j):
        acc_st[i] = jnp.dot(p_buf[0][...], v_ref[tile_rows(j), :], preferred_element_type=F32)

    def value_accumulate(p_buf, i, j):
        al = p_buf[1][...]
        acc_st[i] = jnp.concatenate([al, al], axis=1) * acc_st[i] + jnp.dot(
            p_buf[0][...], v_ref[tile_rows(j), :], preferred_element_type=F32)

    if nq == 1:
        score_tile(s_ring[0], 0, 0)
        softmax_diag(s_ring[0], p_ring[0], 0)
        value_first(p_ring[0], 0, 0)
    else:
        u1 = 8 if nq % 8 == 0 else 4 if nq % 4 == 0 else 2
        assert nq % u1 == 0
        p_b[...] = jnp.zeros(p_b.shape, BF16)
        score_tile(s_ring[0], 0, 0)

        def diag_steps(it, carry):
            for d in range(u1):
                i = it * u1 + d
                cur, other = d % 2, (d + 1) % 2
                nxt = jnp.minimum(i + 1, nq - 1)
                prv = jnp.maximum(i - 1, 0)
                score_tile(s_ring[other], nxt, nxt)
                softmax_diag(s_ring[cur], p_ring[cur], i)
                value_first(p_ring[other], prv, prv)
            return carry

        lax.fori_loop(0, nq // u1, diag_steps, 0)
        value_first(p_ring[(nq - 1) % 2], nq - 1, nq - 1)

    order = [(i, j) for i in range(1, nq) for j in range(i)]
    n_steps = len(order)

    def lower_step(n, cur_ij, prev_ij, next_ij):
        score_tile(s_ring[(n + 1) % 3], *next_ij)
        softmax_lower(s_ring[n % 3], p_ring[n % 3], cur_ij[0])
        value_accumulate(p_ring[(n - 1) % 3], *prev_ij)

    if n_steps > 0:
        p_c[...] = jnp.zeros(p_c.shape, BF16)
        al_c[...] = jnp.ones(al_c.shape, F32)
        score_tile(s_ring[0], 1, 0)

    unrolls = [u for u in (40, 24, 12, 6, 3) if n_steps % u == 0]
    if not unrolls:
        assert n_steps <= 32
        for n, ij in enumerate(order):
            lower_step(n, ij, order[max(n - 1, 0)], order[min(n + 1, n_steps - 1)])
        value_accumulate(p_ring[(n_steps - 1) % 3], *order[-1])
    elif n_steps > 0:
        u2 = unrolls[0]

        def advance(i, j):
            wrap = (j + 1) == i
            return jnp.where(wrap, i + 1, i), jnp.where(wrap, 0, j + 1)

        def softmax_lower_values(s, i):
            p_rows, al_rows = [], []
            for r0 in range(0, t, ATT_ROW_BLOCK):
                rows = pl.ds(r0, ATT_ROW_BLOCK)
                sb = [s[r0:r0 + ATT_ROW_BLOCK, c * lanes:(c + 1) * lanes] for c in range(n_col)]
                m_old = m_st[i, rows, :]
                m_new = jnp.maximum(
                    m_old, jnp.max(functools.reduce(jnp.maximum, sb), axis=-1, keepdims=True))
                p = [jnp.exp2(sc - m_new) for sc in sb]
                alpha = jnp.exp2(m_old - m_new)
                m_st[i, rows, :] = m_new
                p_rows.append(jnp.concatenate([pc.astype(BF16) for pc in p], axis=1))
                al_rows.append(alpha)
            return jnp.concatenate(p_rows, axis=0), jnp.concatenate(al_rows, axis=0)

        def lower_steps(it, carry):
            i, j, ip, jp = carry
            s_cur, p_prev, al_prev = s_a[...], p_c[...], al_c[...]
            for _ in range(u2):
                i_n, j_n = advance(i, j)
                s_nxt = lax.dot_general(q_ref[tile_rows(jnp.minimum(i_n, nq - 1)), :], k_ref[tile_rows(j_n), :],
                                        (((1,), (1,)), ((), ())), preferred_element_type=F32)
                p_cur, al_cur = softmax_lower_values(s_cur, i)
                acc_st[ip] = jnp.concatenate([al_prev, al_prev], axis=1) * acc_st[ip] + jnp.dot(
                    p_prev, v_ref[tile_rows(jp), :], preferred_element_type=F32)
                s_cur, p_prev, al_prev = s_nxt, p_cur, al_cur
                i, j, ip, jp = i_n, j_n, i, j
            s_a[...] = s_cur
            p_c[...] = p_prev
            al_c[...] = al_prev
            return i, j, ip, jp

        one = jnp.int32(1)
        zero = jnp.int32(0)
        _, _, ip, jp = lax.fori_loop(0, n_steps // u2, lower_steps, (one, zero, one, zero))
        value_accumulate(p_ring[2], ip, jp)

    def finish(i, carry):
        acc = acc_st[i]
        gate = g_ref[tile_rows(i), :].astype(F32)
        o_ref[tile_rows(i), :] = (acc[:, :lanes] / acc[:, lanes:] * gate).astype(o_ref.dtype)
        return carry

    lax.fori_loop(0, nq, finish, 0)


def _attention(qp, kp, v, gated):
    B, S, _ = qp.shape
    t = min(S, ATT_TILE)
    nq = S // t
    lanes = V_HEAD_DIM
    head_block = lambda w: pl.BlockSpec((None, S, w), lambda b, h: (b, 0, h))
    return pl.pallas_call(
        _attn_kernel,
        grid=(B, ATT_HEADS),
        in_specs=[head_block(HEAD_PAD), head_block(HEAD_PAD), head_block(V_PAD), head_block(V_HEAD_DIM)],
        out_specs=head_block(V_HEAD_DIM),
        out_shape=jax.ShapeDtypeStruct((B, S, ATT_HEADS * V_HEAD_DIM), BF16),
        scratch_shapes=[pltpu.VMEM((t, t), F32)] * 3
                       + [pltpu.VMEM((t, t), BF16)] * 3
                       + [pltpu.VMEM((t, lanes), F32)] * 3
                       + [
                        pltpu.VMEM((nq, t, lanes), F32),
                        pltpu.VMEM((nq, t, V_PAD), F32)],
        compiler_params=pltpu.CompilerParams(
            dimension_semantics=("arbitrary", "arbitrary"), vmem_limit_bytes=VMEM_LIMIT),
        name="mla_attention",
    )(qp, kp, v, gated)


def _layer_out_kernel(x_ref, y_ref, g_ref, gate_ref, wout_ref, gpost_ref, o_ref):
    aw = y_ref.shape[1]
    y = (jnp.dot(y_ref[...], wout_ref[0:aw, :], preferred_element_type=F32)
         + jnp.dot(g_ref[...], wout_ref[aw:, :], preferred_element_type=F32))
    o_ref[...] = x_ref[...] + gate_ref[...] * _rms(y, gpost_ref[...])


def _layer_out(l, x, y_att, gated, mod, w_out_b, g_post, tm):
    B, S, D = x.shape
    seq = lambda w: pl.BlockSpec((None, tm, w), lambda b, i: (b, i, 0))
    return pl.pallas_call(
        _layer_out_kernel,
        grid=(B, S // tm),
        in_specs=[seq(D), seq(y_att.shape[-1]),
                  pl.BlockSpec((None, tm, gated.shape[-1] - y_att.shape[-1]), lambda b, i: (b, i, 1)),
                  _mod_spec(l, 2, D),
                  _layer_spec(l, w_out_b), _layer_spec(l, g_post)],
        out_specs=seq(D),
        out_shape=jax.ShapeDtypeStruct((B, S, D), F32),
        compiler_params=pltpu.CompilerParams(
            dimension_semantics=("arbitrary", "arbitrary"), vmem_limit_bytes=VMEM_LIMIT),
        name="layer_out",
    )(x, y_att, gated, mod, w_out_b, g_post)


def _prep_w_in(w):
    cut = _H_KR + QK_ROPE_DIM
    pad = jnp.zeros(w.shape[:-1] + (_H_END - cut,), w.dtype)
    return jnp.concatenate([w[..., :cut], pad], axis=-1).astype(BF16), w[..., cut:].astype(BF16)


def _prep_w_uq(w):
    lead = w.shape[:-1]
    w = w.reshape(lead + (ATT_HEADS, QK_NOPE_DIM + QK_ROPE_DIM))
    pad = jnp.zeros(lead + (ATT_HEADS, HEAD_PAD - QK_NOPE_DIM - QK_ROPE_DIM), w.dtype)
    return jnp.concatenate([w, pad], axis=-1).reshape(lead + (ATT_HEADS * HEAD_PAD,)).astype(BF16)


def _prep_w_ukv(w):
    lead = w.shape[:-1]
    w = w.reshape(lead + (ATT_HEADS, QK_NOPE_DIM + V_HEAD_DIM))
    return jnp.concatenate([w[..., :QK_NOPE_DIM].reshape(lead + (-1,)),
                            w[..., QK_NOPE_DIM:].reshape(lead + (-1,))], axis=-1).astype(BF16)


def kernel(x, c, positions, w_ada, b_ada, g_pre, g_post, w_in, q_norm_g, w_uq, kv_norm_g, w_ukv, conv_w,
           conv_b, conv_ln_g, conv_ln_b, w_pw2, sgu_ln_g, sgu_ln_b, w_s, b_s, w_out):
    B, S, D = x.shape
    L = w_ada.shape[0]
    tm = min(S, LAYER_IN_TILE)

    c_pad = jnp.concatenate([c, jnp.zeros((8 - B % 8, D), c.dtype)], axis=0) if B % 8 else c
    mod = _modulation(c_pad, w_ada, b_ada)
    mod = mod.reshape(L, mod.shape[1], 3, 1, D)
    tcos, tsin = _rope_tables(positions)

    vec = lambda a: a.reshape(L, 1, -1)
    conv_w_p = jnp.concatenate([conv_w, jnp.zeros((L, 1, conv_w.shape[-1]), F32)], axis=1)
    bs_wide = jnp.repeat(jnp.swapaxes(b_s, 1, 2), w_pw2.shape[-1] // SGU_GROUPS, axis=2)
    params = (vec(g_pre), *_prep_w_in(w_in), vec(q_norm_g), _prep_w_uq(w_uq), vec(kv_norm_g),
              _prep_w_ukv(w_ukv), conv_w_p, vec(conv_b), vec(conv_ln_g), vec(conv_ln_b),
              w_pw2.astype(BF16), vec(sgu_ln_g), vec(sgu_ln_b), w_s, bs_wide)
    w_out_b = w_out.astype(BF16)
    g_post = vec(g_post)

    for l in range(L):
        qp, kp, v, gated = _layer_in(l, x, mod, tcos, tsin, params, tm)
        y_att = _attention(qp, kp, v, gated)
        x = _layer_out(l, x, y_att, gated, mod, w_out_b, g_post, min(S, LAYER_OUT_TILE))
    return x
```
